```python
import jax, jax.numpy as jnp
from jax import lax
import numpy as np

D_MODEL = 1024
BATCH = 16
SEQ = 2048
DEPTH = 1

NORM_EPS = 1e-6
HG_HEADS = 4
HG_KDIM = 128
HG_VDIM = 128
HG_CHUNK = 64
HG_WIDTH = HG_HEADS * HG_KDIM
ATT_GROUPS = ((128, 1), (512, 4), (2048, 16))
ATT_N_GROUPS = 3
ATT_HEADS_PER_GROUP = 4
ATT_HEAD_DIM = 64
ATT_BLOCK = 128
ATT_WIDTH = ATT_N_GROUPS * ATT_HEADS_PER_GROUP * ATT_HEAD_DIM
ATT_OUT_WIDTH = ATT_HEADS_PER_GROUP * ATT_HEAD_DIM
PEER_HEADS = 8
PEER_NKEYS = 128
PEER_N_EXPERTS = PEER_NKEYS * PEER_NKEYS
PEER_QDIM = 128
PEER_HALF = PEER_QDIM // 2
PEER_TOPK = 16
PEER_TOKEN_BLOCK = 128
IN_COLS = 4 * HG_WIDTH + 3 * ATT_WIDTH + 2 * D_MODEL
IN_SPLITS = (HG_WIDTH, 2 * HG_WIDTH, 3 * HG_WIDTH, 4 * HG_WIDTH,
             4 * HG_WIDTH + ATT_WIDTH, 4 * HG_WIDTH + 2 * ATT_WIDTH,
             4 * HG_WIDTH + 3 * ATT_WIDTH, 4 * HG_WIDTH + 3 * ATT_WIDTH + D_MODEL)

kernel_name = "hybrid_hgrn2_dilated_attn_peer"


def rms_norm(x, g):
    xf = x.astype(jnp.float32)
    y = xf * lax.rsqrt(jnp.mean(xf * xf, axis=-1, keepdims=True) + NORM_EPS)
    return (y * g.astype(jnp.float32)).astype(x.dtype)


def hgrn2_mixer(q_pre, f_pre, i_pre, og_pre, lb, norm_g):
    B, S, _ = q_pre.shape
    f32 = jnp.float32
    lb = lb.astype(f32)
    f = lb + (1.0 - lb) * jax.nn.sigmoid(f_pre.astype(f32))
    log_f = jnp.log(f)
    k = 1.0 - f
    q = jax.nn.sigmoid(q_pre.astype(f32))
    v = i_pre.astype(f32)
    nc = S // HG_CHUNK

    def heads(t):
        return t.reshape(B, nc, HG_CHUNK, HG_HEADS, -1).transpose(0, 3, 1, 2, 4)

    q, k, v, log_f = heads(q), heads(k), heads(v), heads(log_f)
    G = jnp.cumsum(log_f, axis=3)
    G_last = G[:, :, :, -1:, :]
    q_dec = q * jnp.exp(G)
    k_intra = k * jnp.exp(-G)
    k_state = k * jnp.exp(G_last - G)
    causal = jnp.tril(jnp.ones((HG_CHUNK, HG_CHUNK), dtype=bool))
    A = jnp.where(causal, jnp.einsum('bhnck,bhnsk->bhncs', q_dec, k_intra), 0.0)
    o_intra = jnp.einsum('bhncs,bhnsv->bhncv', A, v)

    def step(state, inp):
        q_c, ks_c, v_c, decay_c = inp
        o_c = jnp.einsum('bhck,bhkv->bhcv', q_c, state)
        new_state = decay_c[..., None] * state + jnp.einsum('bhck,bhcv->bhkv', ks_c, v_c)
        return new_state, o_c

    state0 = jnp.zeros((B, HG_HEADS, HG_KDIM, HG_VDIM), f32)
    xs = (jnp.moveaxis(q_dec, 2, 0), jnp.moveaxis(k_state, 2, 0), jnp.moveaxis(v, 2, 0),
          jnp.moveaxis(jnp.exp(G_last[:, :, :, 0, :]), 2, 0))
    _, o_inter = lax.scan(step, state0, xs)
    o = o_intra + jnp.moveaxis(o_inter, 0, 2)
    o = o.transpose(0, 2, 3, 1, 4).reshape(B, S, HG_HEADS, HG_VDIM)
    o = o * lax.rsqrt(jnp.mean(o * o, axis=-1, keepdims=True) + NORM_EPS) * norm_g.astype(f32)
    o = o.reshape(B, S, HG_WIDTH) * jax.nn.silu(og_pre.astype(f32))
    return o.astype(q_pre.dtype)


def dilated_window_attention(q, k, v, window, dilation):
    B, S, H, E = q.shape
    L = S // dilation
    n_blk = -(-L // ATT_BLOCK)
    Lp = n_blk * ATT_BLOCK

    def to_residue(t):
        t = t.reshape(B, L, dilation, H, E).transpose(0, 2, 3, 1, 4)
        return jnp.pad(t, ((0, 0), (0, 0), (0, 0), (0, Lp - L), (0, 0)))

    def with_prev(t):
        tb = t.reshape(B, dilation, H, n_blk, ATT_BLOCK, E)
        prev = jnp.pad(tb[:, :, :, :-1], ((0, 0), (0, 0), (0, 0), (1, 0), (0, 0), (0, 0)))
        return jnp.concatenate([prev, tb], axis=4)

    qb = to_residue(q).reshape(B, dilation, H, n_blk, ATT_BLOCK, E)
    kb = with_prev(to_residue(k))
    vb = with_prev(to_residue(v))
    scores = jnp.einsum('bdhnqe,bdhnke->bdhnqk', qb, kb) * (E ** -0.5)
    n_back = window // dilation
    blk = jnp.arange(n_blk)[:, None, None]
    q_pos = blk * ATT_BLOCK + jnp.arange(ATT_BLOCK)[None, :, None]
    k_pos = (blk - 1) * ATT_BLOCK + jnp.arange(2 * ATT_BLOCK)[None, None, :]
    dist = q_pos - k_pos
    valid = (dist >= 0) & (dist <= n_back) & (k_pos >= 0)
    scores = jnp.where(valid, scores, -jnp.inf)
    m = jnp.max(scores, axis=-1, keepdims=True)
    p = jnp.exp(scores - m)
    l = jnp.sum(p, axis=-1, keepdims=True)
    o = jnp.einsum('bdhnqk,bdhnke->bdhnqe', p, vb) / l

    def from_residue(t):
        X = t.shape[-1]
        t = t.reshape(B, dilation, H, Lp, X)[:, :, :, :L]
        return t.transpose(0, 3, 1, 2, 4).reshape(B, S, H, X)

    return from_residue(o), from_residue(m), from_residue(l)


def dilated_attention_mixer(q_pre, k_pre, v_pre, q_norm_g, k_norm_g):
    B, S, _ = q_pre.shape
    f32 = jnp.float32
    shp = (B, S, ATT_N_GROUPS, ATT_HEADS_PER_GROUP, ATT_HEAD_DIM)

    def head_rms(t, g):
        return t * lax.rsqrt(jnp.mean(t * t, axis=-1, keepdims=True) + NORM_EPS) * g.astype(f32)

    q = head_rms(q_pre.astype(f32).reshape(shp), q_norm_g)
    k = head_rms(k_pre.astype(f32).reshape(shp), k_norm_g)
    v = v_pre.astype(f32).reshape(shp)
    outs = [dilated_window_attention(q[:, :, g], k[:, :, g], v[:, :, g], w, d)
            for g, (w, d) in enumerate(ATT_GROUPS)]
    o_all = jnp.stack([o for o, _, _ in outs])
    m_all = jnp.stack([m for _, m, _ in outs])
    l_all = jnp.stack([l for _, _, l in outs])
    w_all = l_all * jnp.exp(m_all - jnp.max(m_all, axis=0, keepdims=True))
    o = jnp.sum(w_all * o_all, axis=0) / jnp.sum(w_all, axis=0)
    return o.reshape(B, S, ATT_OUT_WIDTH).astype(q_pre.dtype)


def peer_ffn(x, w_q, sub_keys, u, v):
    B, S, D = x.shape
    q = (x @ w_q).reshape(B, S, PEER_HEADS, 2, PEER_HALF)
    s = jnp.einsum('bshpc,hpnc->bshpn', q, sub_keys).astype(jnp.float32)
    s1, i1 = lax.top_k(s[..., 0, :], PEER_TOPK)
    s2, i2 = lax.top_k(s[..., 1, :], PEER_TOPK)
    cand_s = (s1[..., :, None] + s2[..., None, :]).reshape(B, S, PEER_HEADS, PEER_TOPK * PEER_TOPK)
    cand_i = (i1[..., :, None] * PEER_NKEYS + i2[..., None, :]).reshape(B, S, PEER_HEADS, PEER_TOPK * PEER_TOPK)
    top_s, pos = lax.top_k(cand_s, PEER_TOPK)
    idx = jnp.take_along_axis(cand_i, pos, axis=-1)
    gate = jax.nn.softmax(top_s, axis=-1)
    n_blk = (B * S) // PEER_TOKEN_BLOCK
    xt = x.reshape(n_blk, PEER_TOKEN_BLOCK, D)
    it = idx.reshape(n_blk, PEER_TOKEN_BLOCK, PEER_HEADS, PEER_TOPK)
    gt = gate.reshape(n_blk, PEER_TOKEN_BLOCK, PEER_HEADS, PEER_TOPK)

    def block(args):
        xb, ib, gb = args
        hid = jnp.einsum('thkd,td->thk', u[ib], xb)
        a = jax.nn.gelu(hid.astype(jnp.float32), approximate=False) * gb
        return jnp.einsum('thk,thkd->td', a.astype(x.dtype), v[ib])

    out = lax.map(block, (xt, it, gt))
    return out.reshape(B, S, D)


def setup_inputs(seed: int = 0) -> dict:
    key = jax.random.key(seed)
    ks = jax.random.split(key, 16)
    f32 = jnp.float32
    nrm = lambda k, shape, scale: jax.random.normal(k, shape, f32) * scale
    return {
        "x": nrm(ks[0], (BATCH, SEQ, D_MODEL), 1.0),
        "norm1_g": 1.0 + nrm(ks[1], (DEPTH, D_MODEL), 0.02),
        "w_in": nrm(ks[2], (DEPTH, D_MODEL, IN_COLS), D_MODEL ** -0.5),
        "hg_norm_g": 1.0 + nrm(ks[3], (DEPTH, HG_VDIM), 0.02),
        "hg_lb_logits": nrm(ks[4], (DEPTH + 1, HG_WIDTH), 0.02),
        "q_norm_g": 1.0 + nrm(ks[5], (DEPTH, ATT_HEAD_DIM), 0.02),
        "k_norm_g": 1.0 + nrm(ks[6], (DEPTH, ATT_HEAD_DIM), 0.02),
        "w_branch_a": nrm(ks[7], (DEPTH, HG_WIDTH, D_MODEL), HG_WIDTH ** -0.5),
        "w_branch_b": nrm(ks[8], (DEPTH, ATT_OUT_WIDTH, D_MODEL), ATT_OUT_WIDTH ** -0.5),
        "w_out": nrm(ks[9], (DEPTH, D_MODEL, D_MODEL), D_MODEL ** -0.5),
        "norm2_g": 1.0 + nrm(ks[10], (DEPTH, D_MODEL), 0.02),
        "peer_wq": nrm(ks[11], (DEPTH, D_MODEL, PEER_HEADS * PEER_QDIM), D_MODEL ** -0.5),
        "peer_subkeys": nrm(ks[12], (DEPTH, PEER_HEADS, 2, PEER_NKEYS, PEER_HALF), PEER_HALF ** -0.5),
        "peer_u": nrm(ks[13], (DEPTH, PEER_N_EXPERTS, D_MODEL), D_MODEL ** -0.5),
        "peer_v": nrm(ks[14], (DEPTH, PEER_N_EXPERTS, D_MODEL), PEER_HEADS ** -0.5),
    }


def reference(x, norm1_g, w_in, hg_norm_g, hg_lb_logits, q_norm_g, k_norm_g, w_branch_a,
              w_branch_b, w_out, norm2_g, peer_wq, peer_subkeys, peer_u, peer_v):
    lb_all = jnp.cumsum(jax.nn.softmax(hg_lb_logits.astype(jnp.float32), axis=0), axis=0)
    for layer in range(DEPTH):
        h = rms_norm(x, norm1_g[layer])
        proj = h @ w_in[layer]
        hq, hf, hi, hog, aq, ak, av, ga, gb = jnp.split(proj, IN_SPLITS, axis=-1)
        y_a = hgrn2_mixer(hq, hf, hi, hog, lb_all[layer], hg_norm_g[layer])
        y_b = dilated_attention_mixer(aq, ak, av, q_norm_g[layer], k_norm_g[layer])
        merged = (jax.nn.sigmoid(ga) * (y_a @ w_branch_a[layer])
                  + jax.nn.sigmoid(gb) * (y_b @ w_branch_b[layer]))
        x = x + merged @ w_out[layer]
        h2 = rms_norm(x, norm2_g[layer])
        x = x + peer_ffn(h2, peer_wq[layer], peer_subkeys[layer], peer_u[layer], peer_v[layer])
    return x
```

```python
import functools

import jax
import jax.numpy as jnp
from jax import lax
from jax.experimental import pallas as pl
from jax.experimental.pallas import tpu as pltpu

F32 = jnp.float32
BF16 = jnp.bfloat16
I32 = jnp.int32

NORM_EPS = 1e-6
D_MODEL = 1024
HG_HEADS = 4
HG_DIM = 128
HG_CHUNK = 64
HG_WIDTH = HG_HEADS * HG_DIM
ATT_GROUPS = ((128, 1), (512, 4), (2048, 16))
ATT_BLOCK = 128
ATT_HEAD_DIM = 64
ATT_OUT_WIDTH = 256
ATT_WIDTH = 768
PEER_HEADS = 8
PEER_NKEYS = 128
PEER_HALF = 64
PEER_TOPK = 16
PEER_PAIRS = PEER_HEADS * PEER_TOPK
PEER_HALF_D = D_MODEL // 2

COL_GATE_A = 0
COL_GATE_B = D_MODEL
COL_HG = 2 * D_MODEL
COL_ATT = 2 * D_MODEL + 4 * HG_WIDTH
IN_COLS = COL_ATT + 3 * ATT_WIDTH

VMEM_LIMIT = 48 * 1024 * 1024


def _cparams(sem, vmem=VMEM_LIMIT):
    return pltpu.CompilerParams(dimension_semantics=sem, vmem_limit_bytes=vmem)


def _in_proj_kernel(x_ref, g_ref, w_ref, o_ref, h_scr):
    @pl.when(pl.program_id(1) == 0)
    def _():
        x = x_ref[...]
        ms = jnp.mean(x * x, axis=-1, keepdims=True)
        h_scr[...] = (x * lax.rsqrt(ms + NORM_EPS) * g_ref[...]).astype(BF16)

    o_ref[...] = jnp.dot(h_scr[...], w_ref[...], preferred_element_type=F32)


def _in_proj(x2d, g, w_bf16, tm=1024, tn=1280):
    T, D = x2d.shape
    N = w_bf16.shape[1]
    return pl.pallas_call(
        _in_proj_kernel,
        grid=(T // tm, N // tn),
        in_specs=[
            pl.BlockSpec((tm, D), lambda i, j: (i, 0)),
            pl.BlockSpec((1, D), lambda i, j: (0, 0)),
            pl.BlockSpec((D, tn), lambda i, j: (0, j)),
        ],
        out_specs=pl.BlockSpec((tm, tn), lambda i, j: (i, j)),
        out_shape=jax.ShapeDtypeStruct((T, N), F32),
        scratch_shapes=[pltpu.VMEM((tm, D), BF16)],
        compiler_params=_cparams(("parallel", "arbitrary")),
        name="in_proj",
    )(x2d, g.reshape(1, D), w_bf16)


def _split_bf16(x):
    hi = x.astype(BF16)
    lo = (x - hi.astype(F32)).astype(BF16)
    return hi, lo


def _hgrn_kernel(q_ref, f_ref, i_ref, og_ref, lbl_ref, ng_ref, o_ref, state_scr, *, n_chunks):
    @pl.when(pl.program_id(1) == 0)
    def _():
        state_scr[...] = jnp.zeros_like(state_scr)

    logits = lbl_ref[...]
    mx = jnp.max(logits, axis=0, keepdims=True)
    ex = jnp.exp(logits - mx)
    lb = ex[0:1, :] / jnp.sum(ex, axis=0, keepdims=True)
    ng = ng_ref[...]

    C = HG_CHUNK
    row = lax.broadcasted_iota(I32, (C, C), 0)
    col = lax.broadcasted_iota(I32, (C, C), 1)
    causal = row >= col
    tri = causal.astype(BF16)

    def chunk(c, carry):
        r0 = pl.multiple_of(c * C, C)
        rows = pl.ds(r0, C)
        f = lb + (1.0 - lb) * jax.nn.sigmoid(f_ref[rows, :])
        log_f = jnp.log(f)
        k = 1.0 - f
        q = jax.nn.sigmoid(q_ref[rows, :])
        v = i_ref[rows, :]
        og = og_ref[rows, :]
        lf_hi, lf_lo = _split_bf16(log_f)
        G = (jnp.dot(tri, lf_hi, preferred_element_type=F32)
             + jnp.dot(tri, lf_lo, preferred_element_type=F32))
        G_last = G[C - 1:C, :]
        q_dec = (q * jnp.exp(G)).astype(BF16)
        k_intra = (k * jnp.exp(-G)).astype(BF16)
        k_state = (k * jnp.exp(G_last - G)).astype(BF16)
        decay = jnp.exp(G_last)
        v_b = v.astype(BF16)
        for h in range(HG_HEADS):
            sl = slice(h * HG_DIM, (h + 1) * HG_DIM)
            st = state_scr[h]
            A = pl.dot(q_dec[:, sl], k_intra[:, sl], trans_b=True)
            A = jnp.where(causal, A, 0.0)
            o = jnp.dot(A.astype(BF16), v_b[:, sl], preferred_element_type=F32)
            o = o + pl.dot(q_dec[:, sl], st.astype(BF16), trans_b=True)
            state_scr[h] = decay[:, sl] * st + pl.dot(v_b[:, sl], k_state[:, sl], trans_a=True)
            o = o * lax.rsqrt(jnp.mean(o * o, axis=-1, keepdims=True) + NORM_EPS) * ng
            o_ref[rows, sl] = o * jax.nn.silu(og[:, sl])
        return carry

    lax.fori_loop(0, n_chunks, chunk, 0)


def _hgrn(proj, lb_logits, norm_g, B, S, ts=512):
    T = B * S
    nt = S // ts
    blk0 = COL_HG // HG_WIDTH

    def spec(kind):
        return pl.BlockSpec((ts, HG_WIDTH), lambda b, s, kind=kind: (b * nt + s, blk0 + kind))

    return pl.pallas_call(
        functools.partial(_hgrn_kernel, n_chunks=ts // HG_CHUNK),
        grid=(B, nt),
        in_specs=[spec(0), spec(1), spec(2), spec(3),
                  pl.BlockSpec(lb_logits.shape, lambda b, s: (0, 0)),
                  pl.BlockSpec((1, HG_DIM), lambda b, s: (0, 0))],
        out_specs=pl.BlockSpec((ts, HG_WIDTH), lambda b, s: (b * nt + s, 0)),
        out_shape=jax.ShapeDtypeStruct((T, HG_WIDTH), F32),
        scratch_shapes=[pltpu.VMEM((HG_HEADS, HG_DIM, HG_DIM), F32)],
        compiler_params=_cparams(("parallel", "arbitrary")),
        name="hgrn",
    )(proj, proj, proj, proj, lb_logits, norm_g.reshape(1, HG_DIM))


def _att_group(q_ref, k_ref, v_ref, gq, gk, acc_scr, m_scr, l_scr, qn_scr, kp_scr, vp_scr, *, S, d):
    L = S // d
    n_blk = L // ATT_BLOCK
    BLK = ATT_BLOCK
    lane = lax.broadcasted_iota(I32, (1, 2 * ATT_HEAD_DIM), 1)
    head_mask = [(lane < ATT_HEAD_DIM), (lane >= ATT_HEAD_DIM)]
    qi = lax.broadcasted_iota(I32, (BLK, 2 * BLK), 0)
    kj = lax.broadcasted_iota(I32, (BLK, 2 * BLK), 1)
    dist = qi - kj + BLK
    band = (dist >= 0) & (dist <= BLK)
    scale = ATT_HEAD_DIM ** -0.5

    def head_rms(t, g):
        t2 = t * t
        inv = jnp.zeros_like(t)
        for hm in head_mask:
            ms = jnp.sum(jnp.where(hm, t2, 0.0), axis=-1, keepdims=True) * (1.0 / ATT_HEAD_DIM)
            inv = jnp.where(hm, lax.rsqrt(ms + NORM_EPS), inv)
        return t * inv * g

    def residue(r, carry):
        rows_all = pl.ds(r, L, stride=d) if d > 1 else pl.ds(0, L)
        qn_scr[pl.ds(0, L), :] = head_rms(q_ref[rows_all, :], gq).astype(BF16)
        kp_scr[pl.ds(0, BLK), :] = jnp.zeros((BLK, 128), BF16)
        vp_scr[pl.ds(0, BLK), :] = jnp.zeros((BLK, 128), BF16)
        kp_scr[pl.ds(BLK, L), :] = head_rms(k_ref[rows_all, :], gk).astype(BF16)
        vp_scr[pl.ds(BLK, L), :] = v_ref[rows_all, :].astype(BF16)

        def block(n, carry2):
            b0 = pl.multiple_of(n * BLK, BLK)
            valid = band & ((kj >= BLK) | (n > 0))
            if d > 1:
                rows = pl.ds(r + d * b0, BLK, stride=d)
            else:
                rows = pl.ds(b0, BLK)
            qb = qn_scr[pl.ds(b0, BLK), :]
            k2 = kp_scr[pl.ds(b0, 2 * BLK), :]
            v2 = vp_scr[pl.ds(b0, 2 * BLK), :]
            m_new = jnp.zeros((BLK, 128), F32)
            l_new = jnp.zeros((BLK, 128), F32)
            o_new = jnp.zeros((BLK, 128), F32)
            for hm in head_mask:
                qm = jnp.where(hm, qb, jnp.zeros_like(qb))
                s = pl.dot(qm, k2, trans_b=True) * scale
                s = jnp.where(valid, s, -jnp.inf)
                mh = jnp.max(s, axis=-1, keepdims=True)
                p = jnp.exp(s - mh)
                lh = jnp.sum(p, axis=-1, keepdims=True)
                oh = jnp.dot(p.astype(BF16), v2, preferred_element_type=F32)
                m_new = jnp.where(hm, mh, m_new)
                l_new = jnp.where(hm, lh, l_new)
                o_new = jnp.where(hm, oh, o_new)
            m_old = m_scr[rows, :]
            m_tot = jnp.maximum(m_old, m_new)
            a_old = jnp.exp(m_old - m_tot)
            a_new = jnp.exp(m_new - m_tot)
            acc_scr[rows, :] = acc_scr[rows, :] * a_old + o_new * a_new
            l_scr[rows, :] = l_scr[rows, :] * a_old + l_new * a_new
            m_scr[rows, :] = m_tot
            return carry2

        lax.fori_loop(0, n_blk, block, 0)
        return carry

    lax.fori_loop(0, d, residue, 0)


def _attn_kernel(q_ref, k_ref, v_ref, gq_ref, gk_ref, o_ref,
                 acc_scr, m_scr, l_scr, qn_scr, kp_scr, vp_scr, *, S):
    g = pl.program_id(2)

    @pl.when(g == 0)
    def _():
        acc_scr[...] = jnp.zeros_like(acc_scr)
        l_scr[...] = jnp.zeros_like(l_scr)
        m_scr[...] = jnp.full(m_scr.shape, -jnp.inf, F32)

    gq = gq_ref[...]
    gk = gk_ref[...]
    for gi, (w, d) in enumerate(ATT_GROUPS):
        assert w // d == ATT_BLOCK

        @pl.when(g == gi)
        def _(d=d):
            _att_group(q_ref, k_ref, v_ref, gq, gk, acc_scr, m_scr, l_scr,
                       qn_scr, kp_scr, vp_scr, S=S, d=d)

    @pl.when(g == len(ATT_GROUPS) - 1)
    def _():
        o_ref[...] = acc_scr[...] / l_scr[...]


def _attn(proj, q_norm_g, k_norm_g, B, S):
    T = B * S
    W = 2 * ATT_HEAD_DIM
    blk0 = COL_ATT // W
    per_kind = ATT_WIDTH // W
    per_group = ATT_OUT_WIDTH // W

    def spec(kind):
        return pl.BlockSpec((S, W), lambda b, j, g, kind=kind: (b, blk0 + kind * per_kind + g * per_group + j))

    gq2 = jnp.tile(q_norm_g.reshape(1, ATT_HEAD_DIM), (1, 2))
    gk2 = jnp.tile(k_norm_g.reshape(1, ATT_HEAD_DIM), (1, 2))
    return pl.pallas_call(
        functools.partial(_attn_kernel, S=S),
        grid=(B, per_group, len(ATT_GROUPS)),
        in_specs=[spec(0), spec(1), spec(2),
                  pl.BlockSpec((1, W), lambda b, j, g: (0, 0)),
                  pl.BlockSpec((1, W), lambda b, j, g: (0, 0))],
        out_specs=pl.BlockSpec((S, W), lambda b, j, g: (b, j)),
        out_shape=jax.ShapeDtypeStruct((T, ATT_OUT_WIDTH), F32),
        scratch_shapes=[pltpu.VMEM((S, W), F32), pltpu.VMEM((S, W), F32), pltpu.VMEM((S, W), F32),
                        pltpu.VMEM((S, W), BF16),
                        pltpu.VMEM((S + ATT_BLOCK, W), BF16),
                        pltpu.VMEM((S + ATT_BLOCK, W), BF16)],
        compiler_params=_cparams(("parallel", "parallel", "arbitrary")),
        name="attn",
    )(proj, proj, proj, gq2, gk2)


def _merge_kernel(x_ref, ga_ref, gb_ref, ya_ref, yb_ref, wa_ref, wb_ref, wo_ref, g2_ref, wq_ref,
                  x1_ref, h2_ref, qp_ref):
    pa = jnp.dot(ya_ref[...].astype(BF16), wa_ref[...], preferred_element_type=F32)
    pb = jnp.dot(yb_ref[...].astype(BF16), wb_ref[...], preferred_element_type=F32)
    merged = jax.nn.sigmoid(ga_ref[...]) * pa + jax.nn.sigmoid(gb_ref[...]) * pb
    x1 = x_ref[...] + jnp.dot(merged.astype(BF16), wo_ref[...], preferred_element_type=F32)
    x1_ref[...] = x1
    ms = jnp.mean(x1 * x1, axis=-1, keepdims=True)
    h2 = x1 * lax.rsqrt(ms + NORM_EPS) * g2_ref[...]
    h2_ref[...] = h2
    qp_ref[...] = jnp.dot(h2.astype(BF16), wq_ref[...], preferred_element_type=F32)


def _merge(x2d, proj, y_a, y_b, wa, wb, wo, g2, wq, tm=256):
    T, D = x2d.shape
    row = lambda i: (i, 0)
    const = lambda i: (0, 0)
    out_sd = jax.ShapeDtypeStruct((T, D), F32)
    return pl.pallas_call(
        _merge_kernel,
        grid=(T // tm,),
        in_specs=[
            pl.BlockSpec((tm, D), row),
            pl.BlockSpec((tm, D), lambda i: (i, COL_GATE_A // D_MODEL)),
            pl.BlockSpec((tm, D), lambda i: (i, COL_GATE_B // D_MODEL)),
            pl.BlockSpec((tm, HG_WIDTH), row),
            pl.BlockSpec((tm, ATT_OUT_WIDTH), row),
            pl.BlockSpec(wa.shape, const),
            pl.BlockSpec(wb.shape, const),
            pl.BlockSpec(wo.shape, const),
            pl.BlockSpec((1, D), const),
            pl.BlockSpec(wq.shape, const),
        ],
        out_specs=[pl.BlockSpec((tm, D), row)] * 3,
        out_shape=[out_sd, out_sd, out_sd],
        compiler_params=_cparams(("parallel",)),
        name="merge",
    )(x2d, proj, proj, y_a, y_b, wa, wb, wo, g2.reshape(1, D), wq)


def _top16_cols(s, code):
    big = jnp.int32(2 ** 30)
    vals, codes = [], []
    for _ in range(PEER_TOPK):
        m = jnp.max(s, axis=0, keepdims=True)
        c = jnp.min(jnp.where(s == m, code, big), axis=0, keepdims=True)
        s = jnp.where(code == c, -jnp.inf, s)
        vals.append(m)
        codes.append(c)
    return vals, codes


_CAND = [(a, b) for a in range(PEER_TOPK) for b in range(PEER_TOPK) if (a + 1) * (b + 1) <= PEER_TOPK]
_CAND_ROWS = -(-len(_CAND) // 8) * 8


def _peer_topk_kernel(qp_ref, keys_ref, code_ref, idx_ref, gate_ref):
    n = qp_ref.shape[0]
    key_code = lax.broadcasted_iota(I32, (PEER_NKEYS, 1), 0)
    cand_code = code_ref[...]
    pad =[jnp.full((1, n), -jnp.inf, F32)] * (_CAND_ROWS - len(_CAND))
    pad_i = [jnp.zeros((1, n), I32)] * (_CAND_ROWS - len(_CAND))
    for h in range(PEER_HEADS):
        sel = []
        for p in range(2):
            q = qp_ref[:, pl.ds((2 * h + p) * PEER_HALF, PEER_HALF)].astype(BF16)
            s = pl.dot(keys_ref[h, p], q, trans_b=True)
            sel.append(_top16_cols(s, key_code))
        (v1, i1), (v2, i2) = sel
        cand_s = jnp.concatenate([v1[a] + v2[b] for a, b in _CAND] + pad, axis=0)
        cand_i = jnp.concatenate([i1[a] * PEER_NKEYS + i2[b] for a, b in _CAND] + pad_i, axis=0)
        top_s, top_c = _top16_cols(cand_s, cand_code)
        ids = [jnp.sum(jnp.where(cand_code == c, cand_i, 0), axis=0, keepdims=True) for c in top_c]
        ts = jnp.concatenate(top_s, axis=0)
        e = jnp.exp(ts - ts[0:1, :])
        gate = e / jnp.sum(e, axis=0, keepdims=True)
        rows = pl.ds(h * PEER_TOPK, PEER_TOPK)
        idx_ref[rows, :] = jnp.concatenate(ids, axis=0)
        gate_ref[rows, :] = gate


def _peer_topk(qp, keys_bf16, tm=256):
    T, D = qp.shape
    codes = jnp.array([a * PEER_TOPK + b for a, b in _CAND]
                      + [2 ** 20 + i for i in range(_CAND_ROWS - len(_CAND))], I32).reshape(_CAND_ROWS, 1)
    return pl.pallas_call(
        _peer_topk_kernel,
        grid=(T // tm,),
        in_specs=[pl.BlockSpec((tm, D), lambda i: (i, 0)),
                  pl.BlockSpec(keys_bf16.shape, lambda i: (0, 0, 0, 0)),
                  pl.BlockSpec((_CAND_ROWS, 1), lambda i: (0, 0))],
        out_specs=[pl.BlockSpec((PEER_PAIRS, tm), lambda i: (0, i))] * 2,
        out_shape=[jax.ShapeDtypeStruct((PEER_PAIRS, T), I32),
                   jax.ShapeDtypeStruct((PEER_PAIRS, T), F32)],
        compiler_params=_cparams(("parallel",)),
        name="peer_topk",
    )(qp, keys_bf16, codes)


def _pack_table(w):
    b = lax.bitcast_convert_type(w.astype(BF16), jnp.uint16).astype(jnp.uint32)
    packed = b[:, :PEER_HALF_D] | (b[:, PEER_HALF_D:] << 16)
    return lax.bitcast_convert_type(packed, I32).reshape(w.shape[0], 1, PEER_HALF_D)


def _unpack_row(w):
    lo = lax.bitcast_convert_type(w << 16, F32)
    hi = lax.bitcast_convert_type(w & jnp.int32(-65536), F32)
    return lo, hi


def _load_table(tab_hbm, tab_vmem, sem):
    @pl.when(pl.program_id(0) == 0)
    def _():
        cp = pltpu.make_async_copy(tab_hbm, tab_vmem, sem)
        cp.start()
        cp.wait()


def _gelu_exact(x):
    return 0.5 * x * (1.0 + lax.erf(x * (2.0 ** -0.5)))


def _peer_hid_kernel(idx_smem, tab_hbm, xlo_ref, xhi_ref, gate_ref, a_ref,
                     tab_vmem, pbuf, hid_scr, sem, *, tb):
    _load_table(tab_hbm, tab_vmem, sem)
    ones = jnp.ones((PEER_HALF_D, PEER_PAIRS), F32)
    eye = (lax.broadcasted_iota(I32, (PEER_PAIRS, PEER_PAIRS), 0)
           == lax.broadcasted_iota(I32, (PEER_PAIRS, PEER_PAIRS), 1)).astype(F32)

    def token(t, carry):
        xlo = xlo_ref[t]
        xhi = xhi_ref[t]
        for j in range(PEER_PAIRS):
            lo, hi = _unpack_row(tab_vmem[idx_smem[t, j]])
            pbuf[j] = lo * xlo + hi * xhi
        prod = pbuf[...].reshape(PEER_PAIRS, PEER_HALF_D)
        rsum = jnp.dot(prod, ones, preferred_element_type=F32)
        hid_scr[pl.ds(t, 1), :] = jnp.sum(rsum * eye, axis=0, keepdims=True)
        return carry

    lax.fori_loop(0, tb, token, 0)
    a_ref[...] = _gelu_exact(hid_scr[...]) * gate_ref[...]


def _peer_hid(idx, tab_u, xlo, xhi, gate, tb=64):
    T = idx.shape[0]
    n_exp = tab_u.shape[0]
    return pl.pallas_call(
        functools.partial(_peer_hid_kernel, tb=tb),
        grid=(T // tb,),
        in_specs=[pl.BlockSpec((tb, PEER_PAIRS), lambda i: (i, 0), memory_space=pltpu.SMEM),
                  pl.BlockSpec(memory_space=pl.ANY),
                  pl.BlockSpec((tb, 1, PEER_HALF_D), lambda i: (i, 0, 0)),
                  pl.BlockSpec((tb, 1, PEER_HALF_D), lambda i: (i, 0, 0)),
                  pl.BlockSpec((tb, PEER_PAIRS), lambda i: (i, 0))],
        out_specs=pl.BlockSpec((tb, PEER_PAIRS), lambda i: (i, 0)),
        out_shape=jax.ShapeDtypeStruct((T, PEER_PAIRS), F32),
        scratch_shapes=[pltpu.VMEM((n_exp, 1, PEER_HALF_D), I32),
                        pltpu.VMEM((PEER_PAIRS, 1, PEER_HALF_D), F32),
                        pltpu.VMEM((tb, PEER_PAIRS), F32),
                        pltpu.SemaphoreType.DMA(())],
        compiler_params=_cparams(("arbitrary",)),
        name="peer_hid",
    )(idx, tab_u, xlo, xhi, gate)


def _peer_out_kernel(idx_smem, a_smem, tab_hbm, x1_ref, o_ref, tab_vmem, sem, *, tb):
    _load_table(tab_hbm, tab_vmem, sem)
    n_acc = 4

    def token(t, carry):
        acc_lo = [jnp.zeros((1, PEER_HALF_D), F32) for _ in range(n_acc)]
        acc_hi = [jnp.zeros((1, PEER_HALF_D), F32) for _ in range(n_acc)]
        for j in range(PEER_PAIRS):
            lo, hi = _unpack_row(tab_vmem[idx_smem[t, j]])
            a = a_smem[t, j]
            acc_lo[j % n_acc] = acc_lo[j % n_acc] + a * lo
            acc_hi[j % n_acc] = acc_hi[j % n_acc] + a * hi
        lo_sum = (acc_lo[0] + acc_lo[1]) + (acc_lo[2] + acc_lo[3])
        hi_sum = (acc_hi[0] + acc_hi[1]) + (acc_hi[2] + acc_hi[3])
        o_ref[t, :, pl.ds(0, PEER_HALF_D)] = x1_ref[t, :, pl.ds(0, PEER_HALF_D)] + lo_sum
        o_ref[t, :, pl.ds(PEER_HALF_D, PEER_HALF_D)] = x1_ref[t, :, pl.ds(PEER_HALF_D, PEER_HALF_D)] + hi_sum
        return carry

    lax.fori_loop(0, tb, token, 0)


def _peer_out(idx, a, tab_v, x1_3d, tb=64):
    T = idx.shape[0]
    n_exp = tab_v.shape[0]
    return pl.pallas_call(
        functools.partial(_peer_out_kernel, tb=tb),
        grid=(T // tb,),
        in_specs=[pl.BlockSpec((tb, PEER_PAIRS), lambda i: (i, 0), memory_space=pltpu.SMEM),
                  pl.BlockSpec((tb, PEER_PAIRS), lambda i: (i, 0), memory_space=pltpu.SMEM),
                  pl.BlockSpec(memory_space=pl.ANY),
                  pl.BlockSpec((tb, 1, D_MODEL), lambda i: (i, 0, 0))],
        out_specs=pl.BlockSpec((tb, 1, D_MODEL), lambda i: (i, 0, 0)),
        out_shape=jax.ShapeDtypeStruct((T, 1, D_MODEL), F32),
        scratch_shapes=[pltpu.VMEM((n_exp, 1, PEER_HALF_D), I32),
                        pltpu.SemaphoreType.DMA(())],
        compiler_params=_cparams(("arbitrary",)),
        name="peer_out",
    )(idx, a, tab_v, x1_3d)


def _regroup_w_in(w):
    hg, att = 4 * HG_WIDTH, 3 * ATT_WIDTH
    return jnp.concatenate([w[:, hg + att:], w[:, :hg], w[:, hg:hg + att]], axis=1)


def kernel(x, norm1_g, w_in, hg_norm_g, hg_lb_logits, q_norm_g, k_norm_g, w_branch_a, w_branch_b,
           w_out, norm2_g, peer_wq, peer_subkeys, peer_u, peer_v):
    B, S, D = x.shape
    T = B * S
    assert D == D_MODEL and w_in.shape[0] == 1, "single-layer block of width 1024"
    x2d = x.reshape(T, D)

    proj = _in_proj(x2d, norm1_g[0], _regroup_w_in(w_in[0]).astype(BF16))
    y_a = _hgrn(proj, hg_lb_logits, hg_norm_g[0], B, S)
    y_b = _attn(proj, q_norm_g[0], k_norm_g[0], B, S)
    x1, h2, qp = _merge(x2d, proj, y_a, y_b, w_branch_a[0].astype(BF16), w_branch_b[0].astype(BF16),
                        w_out[0].astype(BF16), norm2_g[0], peer_wq[0].astype(BF16))
    idx_t, gate_t = _peer_topk(qp, peer_subkeys[0].astype(BF16))
    idx = idx_t.T
    gate = gate_t.T
    xlo = h2[:, :PEER_HALF_D].reshape(T, 1, PEER_HALF_D)
    xhi = h2[:, PEER_HALF_D:].reshape(T, 1, PEER_HALF_D)
    a = _peer_hid(idx, _pack_table(peer_u[0]), xlo, xhi, gate)
    out = _peer_out(idx, a, _pack_table(peer_v[0]), x1.reshape(T, 1, D))
    return out.reshape(B, S, D)
```

```python
import functools

import jax
import jax.numpy as jnp
from jax import lax
from jax.experimental import pallas as pl
from jax.experimental.pallas import tpu as pltpu

F32 = jnp.float32
BF16 = jnp.bfloat16
I32 = jnp.int32

NORM_EPS = 1e-6
D_MODEL = 1024
HG_HEADS = 4
HG_DIM = 128
HG_CHUNK = 64
HG_WIDTH = HG_HEADS * HG_DIM
ATT_GROUPS = ((128, 1), (512, 4), (2048, 16))
ATT_BLOCK = 128
ATT_HEAD_DIM = 64
ATT_OUT_WIDTH = 256
ATT_WIDTH = 768
PEER_HEADS = 8
PEER_NKEYS = 128
PEER_HALF = 64
PEER_TOPK = 16
PEER_PAIRS = PEER_HEADS * PEER_TOPK
PEER_HALF_D = D_MODEL // 2

COL_GATE_A = 0
COL_GATE_B = D_MODEL
COL_HG = 2 * D_MODEL
COL_ATT = 2 * D_MODEL + 4 * HG_WIDTH
IN_COLS = COL_ATT + 3 * ATT_WIDTH

VMEM_LIMIT = 48 * 1024 * 1024


def _cparams(sem, vmem=VMEM_LIMIT):
    return pltpu.CompilerParams(dimension_semantics=sem, vmem_limit_bytes=vmem)


def _in_proj_kernel(x_ref, g_ref, w_ref, o_ref, h_scr):
    @pl.when(pl.program_id(1) == 0)
    def _():
        x = x_ref[...]
        ms = jnp.mean(x * x, axis=-1, keepdims=True)
        h_scr[...] = (x * lax.rsqrt(ms + NORM_EPS) * g_ref[...]).astype(BF16)

    o_ref[...] = jnp.dot(h_scr[...], w_ref[...], preferred_element_type=F32)


def _in_proj(x2d, g, w_bf16, tm=1024, tn=1280):
    T, D = x2d.shape
    N = w_bf16.shape[1]
    return pl.pallas_call(
        _in_proj_kernel,
        grid=(T // tm, N // tn),
        in_specs=[
            pl.BlockSpec((tm, D), lambda i, j: (i, 0)),
            pl.BlockSpec((1, D), lambda i, j: (0, 0)),
            pl.BlockSpec((D, tn), lambda i, j: (0, j)),
        ],
        out_specs=pl.BlockSpec((tm, tn), lambda i, j: (i, j)),
        out_shape=jax.ShapeDtypeStruct((T, N), F32),
        scratch_shapes=[pltpu.VMEM((tm, D), BF16)],
        compiler_params=_cparams(("parallel", "arbitrary")),
        name="in_proj",
    )(x2d, g.reshape(1, D), w_bf16)


def _split_bf16(x):
    hi = x.astype(BF16)
    lo = (x - hi.astype(F32)).astype(BF16)
    return hi, lo


def _hgrn_kernel(q_ref, f_ref, i_ref, og_ref, lbl_ref, ng_ref, o_ref, state_scr, *, n_chunks):
    @pl.when(pl.program_id(1) == 0)
    def _():
        state_scr[...] = jnp.zeros_like(state_scr)

    logits = lbl_ref[...]
    mx = jnp.max(logits, axis=0, keepdims=True)
    ex = jnp.exp(logits - mx)
    lb = ex[0:1, :] / jnp.sum(ex, axis=0, keepdims=True)
    ng = ng_ref[...]

    C = HG_CHUNK
    row = lax.broadcasted_iota(I32, (C, C), 0)
    col = lax.broadcasted_iota(I32, (C, C), 1)
    causal = row >= col
    tri = causal.astype(BF16)

    def chunk(c, carry):
        r0 = pl.multiple_of(c * C, C)
        rows = pl.ds(r0, C)
        f = lb + (1.0 - lb) * jax.nn.sigmoid(f_ref[rows, :])
        log_f = jnp.log(f)
        k = 1.0 - f
        q = jax.nn.sigmoid(q_ref[rows, :])
        v = i_ref[rows, :]
        og = og_ref[rows, :]
        lf_hi, lf_lo = _split_bf16(log_f)
        G = (jnp.dot(tri, lf_hi, preferred_element_type=F32)
             + jnp.dot(tri, lf_lo, preferred_element_type=F32))
        G_last = G[C - 1:C, :]
        q_dec = (q * jnp.exp(G)).astype(BF16)
        k_intra = (k * jnp.exp(-G)).astype(BF16)
        k_state = (k * jnp.exp(G_last - G)).astype(BF16)
        decay = jnp.exp(G_last)
        v_b = v.astype(BF16)
        for h in range(HG_HEADS):
            sl = slice(h * HG_DIM, (h + 1) * HG_DIM)
            st = state_scr[h]
            A = pl.dot(q_dec[:, sl], k_intra[:, sl], trans_b=True)
            A = jnp.where(causal, A, 0.0)
            o = jnp.dot(A.astype(BF16), v_b[:, sl], preferred_element_type=F32)
            o = o + pl.dot(q_dec[:, sl], st.astype(BF16), trans_b=True)
            state_scr[h] = decay[:, sl] * st + pl.dot(v_b[:, sl], k_state[:, sl], trans_a=True)
            o = o * lax.rsqrt(jnp.mean(o * o, axis=-1, keepdims=True) + NORM_EPS) * ng
            o_ref[rows, sl] = o * jax.nn.silu(og[:, sl])
        return carry

    lax.fori_loop(0, n_chunks, chunk, 0)


def _hgrn(proj, lb_logits, norm_g, B, S, ts=512):
    T = B * S
    nt = S // ts
    blk0 = COL_HG // HG_WIDTH

    def spec(kind):
        return pl.BlockSpec((ts, HG_WIDTH), lambda b, s, kind=kind: (b * nt + s, blk0 + kind))

    return pl.pallas_call(
        functools.partial(_hgrn_kernel, n_chunks=ts // HG_CHUNK),
        grid=(B, nt),
        in_specs=[spec(0), spec(1), spec(2), spec(3),
                  pl.BlockSpec(lb_logits.shape, lambda b, s: (0, 0)),
                  pl.BlockSpec((1, HG_DIM), lambda b, s: (0, 0))],
        out_specs=pl.BlockSpec((ts, HG_WIDTH), lambda b, s: (b * nt + s, 0)),
        out_shape=jax.ShapeDtypeStruct((T, HG_WIDTH), F32),
        scratch_shapes=[pltpu.VMEM((HG_HEADS, HG_DIM, HG_DIM), F32)],
        compiler_params=_cparams(("parallel", "arbitrary")),
        name="hgrn",
    )(proj, proj, proj, proj, lb_logits, norm_g.reshape(1, HG_DIM))


def _att_group(q_ref, k_ref, v_ref, gq, gk, acc_scr, m_scr, l_scr, qn_scr, kp_scr, vp_scr, *, S, d):
    L = S // d
    n_blk = L // ATT_BLOCK
    BLK = ATT_BLOCK
    lane = lax.broadcasted_iota(I32, (1, 2 * ATT_HEAD_DIM), 1)
    head_mask = [(lane < ATT_HEAD_DIM), (lane >= ATT_HEAD_DIM)]
    qi = lax.broadcasted_iota(I32, (BLK, 2 * BLK), 0)
    kj = lax.broadcasted_iota(I32, (BLK, 2 * BLK), 1)
    dist = qi - kj + BLK
    band = (dist >= 0) & (dist <= BLK)
    scale = ATT_HEAD_DIM ** -0.5

    def head_rms(t, g):
        t2 = t * t
        inv = jnp.zeros_like(t)
        for hm in head_mask:
            ms = jnp.sum(jnp.where(hm, t2, 0.0), axis=-1, keepdims=True) * (1.0 / ATT_HEAD_DIM)
            inv = jnp.where(hm, lax.rsqrt(ms + NORM_EPS), inv)
        return t * inv * g

    def residue(r, carry):
        rows_all = pl.ds(r, L, stride=d) if d > 1 else pl.ds(0, L)
        qn_scr[pl.ds(0, L), :] = head_rms(q_ref[rows_all, :], gq).astype(BF16)
        kp_scr[pl.ds(0, BLK), :] = jnp.zeros((BLK, 128), BF16)
        vp_scr[pl.ds(0, BLK), :] = jnp.zeros((BLK, 128), BF16)
        kp_scr[pl.ds(BLK, L), :] = head_rms(k_ref[rows_all, :], gk).astype(BF16)
        vp_scr[pl.ds(BLK, L), :] = v_ref[rows_all, :].astype(BF16)

        def block(n, carry2):
            b0 = pl.multiple_of(n * BLK, BLK)
            valid = band & ((kj >= BLK) | (n > 0))
            if d > 1:
                rows = pl.ds(r + d * b0, BLK, stride=d)
            else:
                rows = pl.ds(b0, BLK)
            qb = qn_scr[pl.ds(b0, BLK), :]
            k2 = kp_scr[pl.ds(b0, 2 * BLK), :]
            v2 = vp_scr[pl.ds(b0, 2 * BLK), :]
            m_new = jnp.zeros((BLK, 128), F32)
            l_new = jnp.zeros((BLK, 128), F32)
            o_new = jnp.zeros((BLK, 128), F32)
            for hm in head_mask:
                qm = jnp.where(hm, qb, jnp.zeros_like(qb))
                s = pl.dot(qm, k2, trans_b=True) * scale
                s = jnp.where(valid, s, -jnp.inf)
                mh = jnp.max(s, axis=-1, keepdims=True)
                p = jnp.exp(s - mh)
                lh = jnp.sum(p, axis=-1, keepdims=True)
                oh = jnp.dot(p.astype(BF16), v2, preferred_element_type=F32)
                m_new = jnp.where(hm, mh, m_new)
                l_new = jnp.where(hm, lh, l_new)
                o_new = jnp.where(hm, oh, o_new)
            m_old = m_scr[rows, :]
            m_tot = jnp.maximum(m_old, m_new)
            a_old = jnp.exp(m_old - m_tot)
            a_new = jnp.exp(m_new - m_tot)
            acc_scr[rows, :] = acc_scr[rows, :] * a_old + o_new * a_new
            l_scr[rows, :] = l_scr[rows, :] * a_old + l_new * a_new
            m_scr[rows, :] = m_tot
            return carry2

        lax.fori_loop(0, n_blk, block, 0)
        return carry

    lax.fori_loop(0, d, residue, 0)


def _attn_kernel(q_ref, k_ref, v_ref, gq_ref, gk_ref, o_ref,
                 acc_scr, m_scr, l_scr, qn_scr, kp_scr, vp_scr, *, S):
    g = pl.program_id(2)

    @pl.when(g == 0)
    def _():
        acc_scr[...] = jnp.zeros_like(acc_scr)
        l_scr[...] = jnp.zeros_like(l_scr)
        m_scr[...] = jnp.full(m_scr.shape, -jnp.inf, F32)

    gq = gq_ref[...]
    gk = gk_ref[...]
    for gi, (w, d) in enumerate(ATT_GROUPS):
        assert w // d == ATT_BLOCK

        @pl.when(g == gi)
        def _(d=d):
            _att_group(q_ref, k_ref, v_ref, gq, gk, acc_scr, m_scr, l_scr,
                       qn_scr, kp_scr, vp_scr, S=S, d=d)

    @pl.when(g == len(ATT_GROUPS) - 1)
    def _():
        o_ref[...] = acc_scr[...] / l_scr[...]


def _attn(proj, q_norm_g, k_norm_g, B, S):
    T = B * S
    W = 2 * ATT_HEAD_DIM
    blk0 = COL_ATT // W
    per_kind = ATT_WIDTH // W
    per_group = ATT_OUT_WIDTH // W

    def spec(kind):
        return pl.BlockSpec((S, W), lambda b, j, g, kind=kind: (b, blk0 + kind * per_kind + g * per_group + j))

    gq2 = jnp.tile(q_norm_g.reshape(1, ATT_HEAD_DIM), (1, 2))
    gk2 = jnp.tile(k_norm_g.reshape(1, ATT_HEAD_DIM), (1, 2))
    return pl.pallas_call(
        functools.partial(_attn_kernel, S=S),
        grid=(B, per_group, len(ATT_GROUPS)),
        in_specs=[spec(0), spec(1), spec(2),
                  pl.BlockSpec((1, W), lambda b, j, g: (0, 0)),
                  pl.BlockSpec((1, W), lambda b, j, g: (0, 0))],
        out_specs=pl.BlockSpec((S, W), lambda b, j, g: (b, j)),
        out_shape=jax.ShapeDtypeStruct((T, ATT_OUT_WIDTH), F32),
        scratch_shapes=[pltpu.VMEM((S, W), F32), pltpu.VMEM((S, W), F32), pltpu.VMEM((S, W), F32),
                        pltpu.VMEM((S, W), BF16),
                        pltpu.VMEM((S + ATT_BLOCK, W), BF16),
                        pltpu.VMEM((S + ATT_BLOCK, W), BF16)],
        compiler_params=_cparams(("parallel", "parallel", "arbitrary")),
        name="attn",
    )(proj, proj, proj, gq2, gk2)


def _merge_kernel(x_ref, ga_ref, gb_ref, ya_ref, yb_ref, wa_ref, wb_ref, wo_ref, g2_ref, wq_ref,
                  x1_ref, h2_ref, qp_ref):
    pa = jnp.dot(ya_ref[...].astype(BF16), wa_ref[...], preferred_element_type=F32)
    pb = jnp.dot(yb_ref[...].astype(BF16), wb_ref[...], preferred_element_type=F32)
    merged = jax.nn.sigmoid(ga_ref[...]) * pa + jax.nn.sigmoid(gb_ref[...]) * pb
    x1 = x_ref[...] + jnp.dot(merged.astype(BF16), wo_ref[...], preferred_element_type=F32)
    x1_ref[...] = x1
    ms = jnp.mean(x1 * x1, axis=-1, keepdims=True)
    h2 = x1 * lax.rsqrt(ms + NORM_EPS) * g2_ref[...]
    h2_ref[...] = h2
    qp_ref[...] = jnp.dot(h2.astype(BF16), wq_ref[...], preferred_element_type=F32)


def _merge(x2d, proj, y_a, y_b, wa, wb, wo, g2, wq, tm=256):
    T, D = x2d.shape
    row = lambda i: (i, 0)
    const = lambda i: (0, 0)
    out_sd = jax.ShapeDtypeStruct((T, D), F32)
    return pl.pallas_call(
        _merge_kernel,
        grid=(T // tm,),
        in_specs=[
            pl.BlockSpec((tm, D), row),
            pl.BlockSpec((tm, D), lambda i: (i, COL_GATE_A // D_MODEL)),
            pl.BlockSpec((tm, D), lambda i: (i, COL_GATE_B // D_MODEL)),
            pl.BlockSpec((tm, HG_WIDTH), row),
            pl.BlockSpec((tm, ATT_OUT_WIDTH), row),
            pl.BlockSpec(wa.shape, const),
            pl.BlockSpec(wb.shape, const),
            pl.BlockSpec(wo.shape, const),
            pl.BlockSpec((1, D), const),
            pl.BlockSpec(wq.shape, const),
        ],
        out_specs=[pl.BlockSpec((tm, D), row)] * 3,
        out_shape=[out_sd, out_sd, out_sd],
        compiler_params=_cparams(("parallel",)),
        name="merge",
    )(x2d, proj, proj, y_a, y_b, wa, wb, wo, g2.reshape(1, D), wq)


def _top16_cols(s, code):
    big = jnp.int32(2 ** 30)
    vals, codes = [], []
    for _ in range(PEER_TOPK):
        m = jnp.max(s, axis=0, keepdims=True)
        c = jnp.min(jnp.where(s == m, code, big), axis=0, keepdims=True)
        s = jnp.where(code == c, -jnp.inf, s)
        vals.append(m)
        codes.append(c)
    return vals, codes


_CAND = [(a, b) for a in range(PEER_TOPK) for b in range(PEER_TOPK) if (a + 1) * (b + 1) <= PEER_TOPK]
_CAND_ROWS = -(-len(_CAND) // 8) * 8


def _peer_topk_kernel(qp_ref, keys_ref, code_ref, idx_ref, gate_ref):
    n = qp_ref.shape[0]
    key_code = lax.broadcasted_iota(I32, (PEER_NKEYS, 1), 0)
    cand_code = code_ref[...]
    pad =[jnp.full((1, n), -jnp.inf, F32)] * (_CAND_ROWS - len(_CAND))
    pad_i = [jnp.zeros((1, n), I32)] * (_CAND_ROWS - len(_CAND))
    for h in range(PEER_HEADS):
        sel = []
        for p in range(2):
            q = qp_ref[:, pl.ds((2 * h + p) * PEER_HALF, PEER_HALF)].astype(BF16)
            s = pl.dot(keys_ref[h, p], q, trans_b=True)
            sel.append(_top16_cols(s, key_code))
        (v1, i1), (v2, i2) = sel
        cand_s = jnp.concatenate([v1[a] + v2[b] for a, b in _CAND] + pad, axis=0)
        cand_i = jnp.concatenate([i1[a] * PEER_NKEYS + i2[b] for a, b in _CAND] + pad_i, axis=0)
        top_s, top_c = _top16_cols(cand_s, cand_code)
        ids = [jnp.sum(jnp.where(cand_code == c, cand_i, 0), axis=0, keepdims=True) for c in top_c]
        ts = jnp.concatenate(top_s, axis=0)
        e = jnp.exp(ts - ts[0:1, :])
        gate = e / jnp.sum(e, axis=0, keepdims=True)
        rows = pl.ds(h * PEER_TOPK, PEER_TOPK)
        idx_ref[rows, :] = jnp.concatenate(ids, axis=0)
        gate_ref[rows, :] = gate


def _peer_topk(qp, keys_bf16, tm=256):
    T, D = qp.shape
    codes = jnp.array([a * PEER_TOPK + b for a, b in _CAND]
                      + [2 ** 20 + i for i in range(_CAND_ROWS - len(_CAND))], I32).reshape(_CAND_ROWS, 1)
    return pl.pallas_call(
        _peer_topk_kernel,
        grid=(T // tm,),
        in_specs=[pl.BlockSpec((tm, D), lambda i: (i, 0)),
                  pl.BlockSpec(keys_bf16.shape, lambda i: (0, 0, 0, 0)),
                  pl.BlockSpec((_CAND_ROWS, 1), lambda i: (0, 0))],
        out_specs=[pl.BlockSpec((PEER_PAIRS, tm), lambda i: (0, i))] * 2,
        out_shape=[jax.ShapeDtypeStruct((PEER_PAIRS, T), I32),
                   jax.ShapeDtypeStruct((PEER_PAIRS, T), F32)],
        compiler_params=_cparams(("parallel",)),
        name="peer_topk",
    )(qp, keys_bf16, codes)


PEER_ROW_WORDS = 4
GATHER_ROWS = PEER_PAIRS * PEER_ROW_WORDS
WIDE = 2 * PEER_PAIRS


def _pack_table(w):
    n = w.shape[0]
    b = lax.bitcast_convert_type(w.astype(BF16), jnp.uint16).astype(jnp.uint32)
    packed = b[:, :PEER_HALF_D] | (b[:, PEER_HALF_D:] << 16)
    return lax.bitcast_convert_type(packed, I32).reshape(n * PEER_ROW_WORDS, 128)


def _table_spec(tab):
    return pl.BlockSpec(tab.shape, lambda i: (0, 0), pipeline_mode=pl.Buffered(1))


def _gather_rows(row_smem, t, tab_ref, gbuf):
    for j in range(PEER_PAIRS):
        r = pl.multiple_of(row_smem[t, j], PEER_ROW_WORDS)
        gbuf[pl.ds(PEER_ROW_WORDS * j, PEER_ROW_WORDS), :] = tab_ref[pl.ds(r, PEER_ROW_WORDS), :]


def _pipelined_tokens(tb, row_smem, tab_ref, gbuf_a, gbuf_b, compute):
    _gather_rows(row_smem, 0, tab_ref, gbuf_a)
    _gather_rows(row_smem, 1, tab_ref, gbuf_b)

    def two_tokens(i, carry):
        t = 2 * i
        compute(t, gbuf_a)
        _gather_rows(row_smem, jnp.minimum(t + 2, tb - 1), tab_ref, gbuf_a)
        compute(t + 1, gbuf_b)
        _gather_rows(row_smem, jnp.minimum(t + 3, tb - 1), tab_ref, gbuf_b)
        return carry

    lax.fori_loop(0, tb // 2, two_tokens, 0)


def _gathered_chunk(gbuf, s):
    return pltpu.bitcast(gbuf[pl.ds(s, PEER_PAIRS, stride=PEER_ROW_WORDS), :], BF16)


def _gelu_exact(x):
    return 0.5 * x * (1.0 + lax.erf(x * (2.0 ** -0.5)))


def _peer_hid_kernel(row_smem, tab_ref, x_ref, gate_ref, a_ref, gbuf_a, gbuf_b, raw_scr, *, tb):
    def token(t, gbuf):
        acc = jnp.zeros((8, WIDE), F32)
        for s in range(PEER_ROW_WORDS):
            xs = x_ref[t, pl.ds(s, 2, stride=PEER_ROW_WORDS), :]
            xs_hi = xs.astype(BF16).astype(F32)
            lhs = jnp.concatenate([xs_hi, xs - xs_hi, jnp.zeros((4, 128), F32)], axis=0).astype(BF16)
            acc = acc + pl.dot(lhs, _gathered_chunk(gbuf, s), trans_b=True)
        raw_scr[t] = acc

    _pipelined_tokens(tb, row_smem, tab_ref, gbuf_a, gbuf_b, token)
    lo = raw_scr[:, 0, :] + raw_scr[:, 2, :]
    hi = raw_scr[:, 1, :] + raw_scr[:, 3, :]
    hid = lo + pltpu.roll(hi, WIDE - 1, 1)
    a_ref[...] = _gelu_exact(hid) * gate_ref[...]


def _peer_hid(rows, tab_u, x3d, gate_wide, tb=128):
    T = rows.shape[0]
    return pl.pallas_call(
        functools.partial(_peer_hid_kernel, tb=tb),
        grid=(T // tb,),
        in_specs=[pl.BlockSpec((tb, PEER_PAIRS), lambda i: (i, 0), memory_space=pltpu.SMEM),
                  _table_spec(tab_u),
                  pl.BlockSpec((tb, 8, 128), lambda i: (i, 0, 0)),
                  pl.BlockSpec((tb, WIDE), lambda i: (i, 0))],
        out_specs=pl.BlockSpec((tb, WIDE), lambda i: (i, 0)),
        out_shape=jax.ShapeDtypeStruct((T, WIDE), F32),
        scratch_shapes=[pltpu.VMEM((GATHER_ROWS, 128), I32),
                        pltpu.VMEM((GATHER_ROWS, 128), I32),
                        pltpu.VMEM((tb, 8, WIDE), F32)],
        compiler_params=_cparams(("arbitrary",)),
        name="peer_hid",
    )(rows, tab_u, x3d, gate_wide)


def _peer_out_kernel(row_smem, tab_ref, a_ref, x1_ref, o_ref, gbuf_a, gbuf_b, *, tb):
    row8 = lax.broadcasted_iota(I32, (8, WIDE), 0)

    def token(t, gbuf):
        a_even = jnp.broadcast_to(a_ref[pl.ds(t, 1), :], (8, WIDE))
        a_odd = pltpu.roll(a_even, 1, 1)
        acc = jnp.zeros((8, 128), F32)
        for s in range(PEER_ROW_WORDS):
            lhs = jnp.where(row8 == s, a_even, jnp.where(row8 == PEER_ROW_WORDS + s, a_odd, 0.0))
            acc = acc + jnp.dot(lhs.astype(BF16), _gathered_chunk(gbuf, s), preferred_element_type=F32)
        o_ref[t] = x1_ref[t] + acc

    _pipelined_tokens(tb, row_smem, tab_ref, gbuf_a, gbuf_b, token)


def _peer_out(rows, a_wide, tab_v, x1_3d, tb=128):
    T = rows.shape[0]
    return pl.pallas_call(
        functools.partial(_peer_out_kernel, tb=tb),
        grid=(T // tb,),
        in_specs=[pl.BlockSpec((tb, PEER_PAIRS), lambda i: (i, 0), memory_space=pltpu.SMEM),
                  _table_spec(tab_v),
                  pl.BlockSpec((tb, WIDE), lambda i: (i, 0)),
                  pl.BlockSpec((tb, 8, 128), lambda i: (i, 0, 0))],
        out_specs=pl.BlockSpec((tb, 8, 128), lambda i: (i, 0, 0)),
        out_shape=jax.ShapeDtypeStruct((T, 8, 128), F32),
        scratch_shapes=[pltpu.VMEM((GATHER_ROWS, 128), I32),
                        pltpu.VMEM((GATHER_ROWS, 128), I32)],
        compiler_params=_cparams(("arbitrary",)),
        name="peer_out",
    )(rows, tab_v, a_wide, x1_3d)


def _regroup_w_in(w):
    hg, att = 4 * HG_WIDTH, 3 * ATT_WIDTH
    return jnp.concatenate([w[:, hg + att:], w[:, :hg], w[:, hg:hg + att]], axis=1)


def kernel(x, norm1_g, w_in, hg_norm_g, hg_lb_logits, q_norm_g, k_norm_g, w_branch_a, w_branch_b,
           w_out, norm2_g, peer_wq, peer_subkeys, peer_u, peer_v):
    B, S, D = x.shape
    T = B * S
    assert D == D_MODEL and w_in.shape[0] == 1, "single-layer block of width 1024"
    x2d = x.reshape(T, D)

    proj = _in_proj(x2d, norm1_g[0], _regroup_w_in(w_in[0]).astype(BF16))
    y_a = _hgrn(proj, hg_lb_logits, hg_norm_g[0], B, S)
    y_b = _attn(proj, q_norm_g[0], k_norm_g[0], B, S)
    x1, h2, qp = _merge(x2d, proj, y_a, y_b, w_branch_a[0].astype(BF16), w_branch_b[0].astype(BF16),
                        w_out[0].astype(BF16), norm2_g[0], peer_wq[0].astype(BF16))
    idx_t, gate_t = _peer_topk(qp, peer_subkeys[0].astype(BF16))
    rows = idx_t.T * PEER_ROW_WORDS
    gate = gate_t.T
    gate_wide = jnp.stack([gate, jnp.zeros_like(gate)], axis=-1).reshape(T, WIDE)
    a_wide = _peer_hid(rows, _pack_table(peer_u[0]), h2.reshape(T, 8, 128), gate_wide)
    out = _peer_out(rows, a_wide, _pack_table(peer_v[0]), x1.reshape(T, 8, 128))
    return out.reshape(B, S, D)
```

```python
import functools

import jax
import jax.numpy as jnp
from jax import lax
from jax.experimental import pallas as pl
from jax.experimental.pallas import tpu as pltpu

F32 = jnp.float32
BF16 = jnp.bfloat16
I32 = jnp.int32

NORM_EPS = 1e-6
D_MODEL = 1024
HG_HEADS = 4
HG_DIM = 128
HG_CHUNK = 64
HG_WIDTH = HG_HEADS * HG_DIM
ATT_GROUPS = ((128, 1), (512, 4), (2048, 16))
ATT_BLOCK = 128
ATT_INTERLEAVE = 4
ATT_HEAD_DIM = 64
ATT_OUT_WIDTH = 256
ATT_WIDTH = 768
PEER_HEADS = 8
PEER_NKEYS = 128
PEER_HALF = 64
PEER_TOPK = 16
PEER_PAIRS = PEER_HEADS * PEER_TOPK
PEER_HALF_D = D_MODEL // 2

COL_GATE_A = 0
COL_GATE_B = D_MODEL
COL_HG = 2 * D_MODEL
COL_ATT = 2 * D_MODEL + 4 * HG_WIDTH
IN_COLS = COL_ATT + 3 * ATT_WIDTH

VMEM_LIMIT = 48 * 1024 * 1024


def _cparams(sem, vmem=VMEM_LIMIT):
    return pltpu.CompilerParams(dimension_semantics=sem, vmem_limit_bytes=vmem)


def _in_proj_kernel(x_ref, g_ref, w_ref, o_ref, h_scr):
    @pl.when(pl.program_id(1) == 0)
    def _():
        x = x_ref[...]
        ms = jnp.mean(x * x, axis=-1, keepdims=True)
        h_scr[...] = (x * lax.rsqrt(ms + NORM_EPS) * g_ref[...]).astype(BF16)

    o_ref[...] = jnp.dot(h_scr[...], w_ref[...], preferred_element_type=F32)


def _in_proj(x2d, g, w_bf16, tm=1024, tn=1280):
    T, D = x2d.shape
    N = w_bf16.shape[1]
    return pl.pallas_call(
        _in_proj_kernel,
        grid=(T // tm, N // tn),
        in_specs=[
            pl.BlockSpec((tm, D), lambda i, j: (i, 0)),
            pl.BlockSpec((1, D), lambda i, j: (0, 0)),
            pl.BlockSpec((D, tn), lambda i, j: (0, j)),
        ],
        out_specs=pl.BlockSpec((tm, tn), lambda i, j: (i, j)),
        out_shape=jax.ShapeDtypeStruct((T, N), F32),
        scratch_shapes=[pltpu.VMEM((tm, D), BF16)],
        compiler_params=_cparams(("parallel", "arbitrary")),
        name="in_proj",
    )(x2d, g.reshape(1, D), w_bf16)


def _split_bf16(x):
    hi = x.astype(BF16)
    lo = (x - hi.astype(F32)).astype(BF16)
    return hi, lo


def _hgrn_kernel(q_ref, f_ref, i_ref, og_ref, lbl_ref, ng_ref, o_ref, state_scr, *, n_chunks):
    @pl.when(pl.program_id(1) == 0)
    def _():
        state_scr[...] = jnp.zeros_like(state_scr)

    logits = lbl_ref[...]
    mx = jnp.max(logits, axis=0, keepdims=True)
    ex = jnp.exp(logits - mx)
    lb = ex[0:1, :] / jnp.sum(ex, axis=0, keepdims=True)
    ng = ng_ref[...]

    C = HG_CHUNK
    row = lax.broadcasted_iota(I32, (C, C), 0)
    col = lax.broadcasted_iota(I32, (C, C), 1)
    causal = row >= col
    tri = causal.astype(BF16)

    def chunk(c, carry):
        r0 = pl.multiple_of(c * C, C)
        rows = pl.ds(r0, C)
        f = lb + (1.0 - lb) * jax.nn.sigmoid(f_ref[rows, :])
        log_f = jnp.log(f)
        k = 1.0 - f
        q = jax.nn.sigmoid(q_ref[rows, :])
        v = i_ref[rows, :]
        og = og_ref[rows, :]
        lf_hi, lf_lo = _split_bf16(log_f)
        G = (jnp.dot(tri, lf_hi, preferred_element_type=F32)
             + jnp.dot(tri, lf_lo, preferred_element_type=F32))
        G_last = G[C - 1:C, :]
        q_dec = (q * jnp.exp(G)).astype(BF16)
        k_intra = (k * jnp.exp(-G)).astype(BF16)
        k_state = (k * jnp.exp(G_last - G)).astype(BF16)
        decay = jnp.exp(G_last)
        v_b = v.astype(BF16)
        for h in range(HG_HEADS):
            sl = slice(h * HG_DIM, (h + 1) * HG_DIM)
            st = state_scr[h]
            A = pl.dot(q_dec[:, sl], k_intra[:, sl], trans_b=True)
            A = jnp.where(causal, A, 0.0)
            o = jnp.dot(A.astype(BF16), v_b[:, sl], preferred_element_type=F32)
            o = o + pl.dot(q_dec[:, sl], st.astype(BF16), trans_b=True)
            state_scr[h] = decay[:, sl] * st + pl.dot(v_b[:, sl], k_state[:, sl], trans_a=True)
            o = o * lax.rsqrt(jnp.mean(o * o, axis=-1, keepdims=True) + NORM_EPS) * ng
            o_ref[rows, sl] = o * jax.nn.silu(og[:, sl])
        return carry

    lax.fori_loop(0, n_chunks, chunk, 0)


def _hgrn(proj, lb_logits, norm_g, B, S, ts=512):
    T = B * S
    nt = S // ts
    blk0 = COL_HG // HG_WIDTH

    def spec(kind):
        return pl.BlockSpec((ts, HG_WIDTH), lambda b, s, kind=kind: (b * nt + s, blk0 + kind))

    return pl.pallas_call(
        functools.partial(_hgrn_kernel, n_chunks=ts // HG_CHUNK),
        grid=(B, nt),
        in_specs=[spec(0), spec(1), spec(2), spec(3),
                  pl.BlockSpec(lb_logits.shape, lambda b, s: (0, 0)),
                  pl.BlockSpec((1, HG_DIM), lambda b, s: (0, 0))],
        out_specs=pl.BlockSpec((ts, HG_WIDTH), lambda b, s: (b * nt + s, 0)),
        out_shape=jax.ShapeDtypeStruct((T, HG_WIDTH), F32),
        scratch_shapes=[pltpu.VMEM((HG_HEADS, HG_DIM, HG_DIM), F32)],
        compiler_params=_cparams(("parallel", "arbitrary")),
        name="hgrn",
    )(proj, proj, proj, proj, lb_logits, norm_g.reshape(1, HG_DIM))


def _att_group(q_ref, k_ref, v_ref, gq, gk, acc_scr, m_scr, l_scr, qn_scr, kp_scr, vp_scr, *, S, d):
    L = S // d
    n_blk = L // ATT_BLOCK
    BLK = ATT_BLOCK
    lane = lax.broadcasted_iota(I32, (1, 2 * ATT_HEAD_DIM), 1)
    head_mask = [(lane < ATT_HEAD_DIM), (lane >= ATT_HEAD_DIM)]
    qi = lax.broadcasted_iota(I32, (BLK, 2 * BLK), 0)
    kj = lax.broadcasted_iota(I32, (BLK, 2 * BLK), 1)
    dist = qi - kj + BLK
    band = (dist >= 0) & (dist <= BLK)
    scale = ATT_HEAD_DIM ** -0.5

    def head_rms(t, g):
        t2 = t * t
        inv = jnp.zeros_like(t)
        for hm in head_mask:
            ms = jnp.sum(jnp.where(hm, t2, 0.0), axis=-1, keepdims=True) * (1.0 / ATT_HEAD_DIM)
            inv = jnp.where(hm, lax.rsqrt(ms + NORM_EPS), inv)
        return t * inv * g

    def prepare(r, slot):
        rows_all = pl.ds(r, L, stride=d) if d > 1 else pl.ds(0, L)
        k0 = slot * (L + BLK)
        qn_scr[pl.ds(slot * L, L), :] = head_rms(q_ref[rows_all, :], gq).astype(BF16)
        kp_scr[pl.ds(k0, BLK), :] = jnp.zeros((BLK, 128), BF16)
        vp_scr[pl.ds(k0, BLK), :] = jnp.zeros((BLK, 128), BF16)
        kp_scr[pl.ds(k0 + BLK, L), :] = head_rms(k_ref[rows_all, :], gk).astype(BF16)
        vp_scr[pl.ds(k0 + BLK, L), :] = v_ref[rows_all, :].astype(BF16)

    def block_stats(slot, n):
        b0 = n * BLK
        if not isinstance(n, int):
            b0 = pl.multiple_of(b0, BLK)
        valid = band & ((kj >= BLK) | (n > 0))
        qb = qn_scr[pl.ds(slot * L + b0, BLK), :]
        k2 = kp_scr[pl.ds(slot * (L + BLK) + b0, 2 * BLK), :]
        v2 = vp_scr[pl.ds(slot * (L + BLK) + b0, 2 * BLK), :]
        m_new = jnp.zeros((BLK, 128), F32)
        l_new = jnp.zeros((BLK, 128), F32)
        o_new = jnp.zeros((BLK, 128), F32)
        for hm in head_mask:
            qm = jnp.where(hm, qb, jnp.zeros_like(qb))
            s = pl.dot(qm, k2, trans_b=True) * scale
            s = jnp.where(valid, s, -jnp.inf)
            mh = jnp.max(s, axis=-1, keepdims=True)
            p = jnp.exp(s - mh)
            lh = jnp.sum(p, axis=-1, keepdims=True)
            oh = jnp.dot(p.astype(BF16), v2, preferred_element_type=F32)
            m_new = jnp.where(hm, mh, m_new)
            l_new = jnp.where(hm, lh, l_new)
            o_new = jnp.where(hm, oh, o_new)
        return m_new, l_new, o_new

    def fold(items):
        stats = [block_stats(slot, n) for _, slot, n in items]
        rows = [pl.ds(r + d * n * BLK, BLK, stride=d) if d > 1 else pl.ds(n * BLK, BLK)
                for r, _, n in items]
        old = [(m_scr[rw, :], l_scr[rw, :], acc_scr[rw, :]) for rw in rows]
        for rw, (m_new, l_new, o_new), (m_old, l_old, acc_old) in zip(rows, stats, old):
            m_tot = jnp.maximum(m_old, m_new)
            a_old = jnp.exp(m_old - m_tot)
            a_new = jnp.exp(m_new - m_tot)
            acc_scr[rw, :] = acc_old * a_old + o_new * a_new
            l_scr[rw, :] = l_old * a_old + l_new * a_new
            m_scr[rw, :] = m_tot

    U = ATT_INTERLEAVE
    if n_blk >= U:
        def residue(r, carry):
            prepare(r, 0)

            def blocks(i, carry2):
                fold([(r, 0, i * U + u) for u in range(U)])
                return carry2

            lax.fori_loop(0, n_blk // U, blocks, 0)
            return carry

        lax.fori_loop(0, d, residue, 0)
    else:
        per_step = U // n_blk

        def residues(i, carry):
            for u in range(per_step):
                prepare(i * per_step + u, u)
            fold([(i * per_step + u, u, n) for u in range(per_step) for n in range(n_blk)])
            return carry

        lax.fori_loop(0, d // per_step, residues, 0)


def _attn_kernel(q_ref, k_ref, v_ref, gq_ref, gk_ref, o_ref,
                 acc_scr, m_scr, l_scr, qn_scr, kp_scr, vp_scr, *, S):
    g = pl.program_id(2)

    @pl.when(g == 0)
    def _():
        acc_scr[...] = jnp.zeros_like(acc_scr)
        l_scr[...] = jnp.zeros_like(l_scr)
        m_scr[...] = jnp.full(m_scr.shape, -jnp.inf, F32)

    gq = gq_ref[...]
    gk = gk_ref[...]
    for gi, (w, d) in enumerate(ATT_GROUPS):
        assert w // d == ATT_BLOCK

        @pl.when(g == gi)
        def _(d=d):
            _att_group(q_ref, k_ref, v_ref, gq, gk, acc_scr, m_scr, l_scr,
                       qn_scr, kp_scr, vp_scr, S=S, d=d)

    @pl.when(g == len(ATT_GROUPS) - 1)
    def _():
        o_ref[...] = acc_scr[...] / l_scr[...]


def _attn(proj, q_norm_g, k_norm_g, B, S):
    T = B * S
    W = 2 * ATT_HEAD_DIM
    blk0 = COL_ATT // W
    per_kind = ATT_WIDTH // W
    per_group = ATT_OUT_WIDTH // W

    def spec(kind):
        return pl.BlockSpec((S, W), lambda b, j, g, kind=kind: (b, blk0 + kind * per_kind + g * per_group + j))

    gq2 = jnp.tile(q_norm_g.reshape(1, ATT_HEAD_DIM), (1, 2))
    gk2 = jnp.tile(k_norm_g.reshape(1, ATT_HEAD_DIM), (1, 2))
    return pl.pallas_call(
        functools.partial(_attn_kernel, S=S),
        grid=(B, per_group, len(ATT_GROUPS)),
        in_specs=[spec(0), spec(1), spec(2),
                  pl.BlockSpec((1, W), lambda b, j, g: (0, 0)),
                  pl.BlockSpec((1, W), lambda b, j, g: (0, 0))],
        out_specs=pl.BlockSpec((S, W), lambda b, j, g: (b, j)),
        out_shape=jax.ShapeDtypeStruct((T, ATT_OUT_WIDTH), F32),
        scratch_shapes=[pltpu.VMEM((S, W), F32), pltpu.VMEM((S, W), F32), pltpu.VMEM((S, W), F32),
                        pltpu.VMEM((S, W), BF16),
                        pltpu.VMEM((S + ATT_BLOCK, W), BF16),
                        pltpu.VMEM((S + ATT_BLOCK, W), BF16)],
        compiler_params=_cparams(("parallel", "parallel", "arbitrary")),
        name="attn",
    )(proj, proj, proj, gq2, gk2)


def _merge_kernel(x_ref, ga_ref, gb_ref, ya_ref, yb_ref, wa_ref, wb_ref, wo_ref, g2_ref, wq_ref,
                  x1_ref, h2_ref, qp_ref):
    pa = jnp.dot(ya_ref[...].astype(BF16), wa_ref[...], preferred_element_type=F32)
    pb = jnp.dot(yb_ref[...].astype(BF16), wb_ref[...], preferred_element_type=F32)
    merged = jax.nn.sigmoid(ga_ref[...]) * pa + jax.nn.sigmoid(gb_ref[...]) * pb
    x1 = x_ref[...] + jnp.dot(merged.astype(BF16), wo_ref[...], preferred_element_type=F32)
    x1_ref[...] = x1
    ms = jnp.mean(x1 * x1, axis=-1, keepdims=True)
    h2 = x1 * lax.rsqrt(ms + NORM_EPS) * g2_ref[...]
    h2_ref[...] = h2
    qp_ref[...] = jnp.dot(h2.astype(BF16), wq_ref[...], preferred_element_type=F32)


def _merge(x2d, proj, y_a, y_b, wa, wb, wo, g2, wq, tm=256):
    T, D = x2d.shape
    row = lambda i: (i, 0)
    const = lambda i: (0, 0)
    out_sd = jax.ShapeDtypeStruct((T, D), F32)
    return pl.pallas_call(
        _merge_kernel,
        grid=(T // tm,),
        in_specs=[
            pl.BlockSpec((tm, D), row),
            pl.BlockSpec((tm, D), lambda i: (i, COL_GATE_A // D_MODEL)),
            pl.BlockSpec((tm, D), lambda i: (i, COL_GATE_B // D_MODEL)),
            pl.BlockSpec((tm, HG_WIDTH), row),
            pl.BlockSpec((tm, ATT_OUT_WIDTH), row),
            pl.BlockSpec(wa.shape, const),
            pl.BlockSpec(wb.shape, const),
            pl.BlockSpec(wo.shape, const),
            pl.BlockSpec((1, D), const),
            pl.BlockSpec(wq.shape, const),
        ],
        out_specs=[pl.BlockSpec((tm, D), row)] * 3,
        out_shape=[out_sd, out_sd, out_sd],
        compiler_params=_cparams(("parallel",)),
        name="merge",
    )(x2d, proj, proj, y_a, y_b, wa, wb, wo, g2.reshape(1, D), wq)


def _top16_cols(s, code):
    big = jnp.int32(2 ** 30)
    vals, codes = [], []
    for _ in range(PEER_TOPK):
        m = jnp.max(s, axis=0, keepdims=True)
        c = jnp.min(jnp.where(s == m, code, big), axis=0, keepdims=True)
        s = jnp.where(code == c, -jnp.inf, s)
        vals.append(m)
        codes.append(c)
    return vals, codes


_CAND = [(a, b) for a in range(PEER_TOPK) for b in range(PEER_TOPK) if (a + 1) * (b + 1) <= PEER_TOPK]
_CAND_ROWS = -(-len(_CAND) // 8) * 8


def _peer_topk_kernel(qp_ref, keys_ref, code_ref, idx_ref, gate_ref):
    n = qp_ref.shape[0]
    key_code = lax.broadcasted_iota(I32, (PEER_NKEYS, 1), 0)
    cand_code = code_ref[...]
    pad =[jnp.full((1, n), -jnp.inf, F32)] * (_CAND_ROWS - len(_CAND))
    pad_i = [jnp.zeros((1, n), I32)] * (_CAND_ROWS - len(_CAND))
    for h in range(PEER_HEADS):
        sel = []
        for p in range(2):
            q = qp_ref[:, pl.ds((2 * h + p) * PEER_HALF, PEER_HALF)].astype(BF16)
            s = pl.dot(keys_ref[h, p], q, trans_b=True)
            sel.append(_top16_cols(s, key_code))
        (v1, i1), (v2, i2) = sel
        cand_s = jnp.concatenate([v1[a] + v2[b] for a, b in _CAND] + pad, axis=0)
        cand_i = jnp.concatenate([i1[a] * PEER_NKEYS + i2[b] for a, b in _CAND] + pad_i, axis=0)
        top_s, top_c = _top16_cols(cand_s, cand_code)
        ids = [jnp.sum(jnp.where(cand_code == c, cand_i, 0), axis=0, keepdims=True) for c in top_c]
        ts = jnp.concatenate(top_s, axis=0)
        e = jnp.exp(ts - ts[0:1, :])
        gate = e / jnp.sum(e, axis=0, keepdims=True)
        rows = pl.ds(h * PEER_TOPK, PEER_TOPK)
        idx_ref[rows, :] = jnp.concatenate(ids, axis=0)
        gate_ref[rows, :] = gate


def _peer_topk(qp, keys_bf16, tm=256):
    T, D = qp.shape
    codes = jnp.array([a * PEER_TOPK + b for a, b in _CAND]
                      + [2 ** 20 + i for i in range(_CAND_ROWS - len(_CAND))], I32).reshape(_CAND_ROWS, 1)
    return pl.pallas_call(
        _peer_topk_kernel,
        grid=(T // tm,),
        in_specs=[pl.BlockSpec((tm, D), lambda i: (i, 0)),
                  pl.BlockSpec(keys_bf16.shape, lambda i: (0, 0, 0, 0)),
                  pl.BlockSpec((_CAND_ROWS, 1), lambda i: (0, 0))],
        out_specs=[pl.BlockSpec((PEER_PAIRS, tm), lambda i: (0, i))] * 2,
        out_shape=[jax.ShapeDtypeStruct((PEER_PAIRS, T), I32),
                   jax.ShapeDtypeStruct((PEER_PAIRS, T), F32)],
        compiler_params=_cparams(("parallel",)),
        name="peer_topk",
    )(qp, keys_bf16, codes)


PEER_ROW_WORDS = 4
GATHER_ROWS = PEER_PAIRS * PEER_ROW_WORDS
WIDE = 2 * PEER_PAIRS
PEER_CHUNK = 32


def _pack_table(w):
    n = w.shape[0]

    def bf16_bits(part):
        return lax.bitcast_convert_type(part.astype(BF16).astype(F32), jnp.uint32)

    packed = (bf16_bits(w[:, :PEER_HALF_D]) >> 16) | bf16_bits(w[:, PEER_HALF_D:])
    return lax.bitcast_convert_type(packed, I32).reshape(n * PEER_ROW_WORDS, 128)


def _table_spec(tab):
    return pl.BlockSpec(tab.shape, lambda i: (0, 0), pipeline_mode=pl.Buffered(1))


def _gather_rows(row_smem, t, tab_ref, gbuf):
    for j in range(PEER_PAIRS):
        r = pl.multiple_of(row_smem[t, j], PEER_ROW_WORDS)
        gbuf[pl.ds(PEER_ROW_WORDS * j, PEER_ROW_WORDS), :] = tab_ref[pl.ds(r, PEER_ROW_WORDS), :]


def _peer_step(rows_hbm, tab_ref, row_smem, sems, gbufs, compute):
    C = PEER_CHUNK
    i = pl.program_id(0)

    def fetch(chunk, slot):
        return pltpu.make_async_copy(rows_hbm.at[pl.ds(chunk * C, C)], row_smem[slot], sems.at[slot])

    @pl.when(i == 0)
    def _():
        fetch(0, 0).start()

    fetch(2 * i, 0).wait()
    fetch(2 * i + 1, 1).start()

    def gather(k):
        _gather_rows(row_smem[k // C], k % C, tab_ref, gbufs[k % 2])

    gather(0)
    gather(1)
    for k in range(2 * C):
        compute(k, gbufs[k % 2])
        nxt = k + 2
        if nxt == C:
            fetch(2 * i + 1, 1).wait()
        if nxt < 2 * C:
            gather(nxt)
        if nxt == C + 1:
            @pl.when(i + 1 < pl.num_programs(0))
            def _():
                fetch(2 * i + 2, 0).start()


def _gathered_chunk(gbuf, s):
    return pltpu.bitcast(gbuf[pl.ds(s, PEER_PAIRS, stride=PEER_ROW_WORDS), :], BF16)


def _gelu_exact(x):
    return 0.5 * x * (1.0 + lax.erf(x * (2.0 ** -0.5)))


def _peer_hid_kernel(rows_hbm, tab_ref, x_ref, gate_ref, a_ref, smem_a, smem_b, sems, gbuf_a, gbuf_b, raw_scr):
    def token(t, gbuf):
        acc = jnp.zeros((8, WIDE), F32)
        for s in range(PEER_ROW_WORDS):
            xs = x_ref[t, pl.ds(s, 2, stride=PEER_ROW_WORDS), :]
            xs_hi = xs.astype(BF16).astype(F32)
            lhs = jnp.concatenate([xs_hi, xs - xs_hi, jnp.zeros((4, 128), F32)], axis=0).astype(BF16)
            acc = acc + pl.dot(lhs, _gathered_chunk(gbuf, s), trans_b=True)
        raw_scr[t] = acc

    _peer_step(rows_hbm, tab_ref, (smem_a, smem_b), sems, (gbuf_a, gbuf_b), token)
    lo = raw_scr[:, 0, :] + raw_scr[:, 2, :]
    hi = raw_scr[:, 1, :] + raw_scr[:, 3, :]
    hid = lo + pltpu.roll(hi, WIDE - 1, 1)
    a_ref[...] = _gelu_exact(hid) * gate_ref[...]


def _peer_scratch():
    return [pltpu.SMEM((PEER_CHUNK, PEER_PAIRS), I32), pltpu.SMEM((PEER_CHUNK, PEER_PAIRS), I32),
            pltpu.SemaphoreType.DMA((2,)),
            pltpu.VMEM((GATHER_ROWS, 128), I32), pltpu.VMEM((GATHER_ROWS, 128), I32)]


def _peer_hid(rows, tab_u, x3d, gate_wide):
    T = rows.shape[0]
    tb = 2 * PEER_CHUNK
    return pl.pallas_call(
        _peer_hid_kernel,
        grid=(T // tb,),
        in_specs=[pl.BlockSpec(memory_space=pl.ANY),
                  _table_spec(tab_u),
                  pl.BlockSpec((tb, 8, 128), lambda i: (i, 0, 0)),
                  pl.BlockSpec((tb, WIDE), lambda i: (i, 0))],
        out_specs=pl.BlockSpec((tb, WIDE), lambda i: (i, 0)),
        out_shape=jax.ShapeDtypeStruct((T, WIDE), F32),
        scratch_shapes=_peer_scratch() + [pltpu.VMEM((tb, 8, WIDE), F32)],
        compiler_params=_cparams(("arbitrary",)),
        name="peer_hid",
    )(rows, tab_u, x3d, gate_wide)


def _peer_out_kernel(rows_hbm, tab_ref, a_ref, x1_ref, o_ref, smem_a, smem_b, sems, gbuf_a, gbuf_b):
    row8 = lax.broadcasted_iota(I32, (8, WIDE), 0)

    def token(t, gbuf):
        a_even = jnp.broadcast_to(a_ref[pl.ds(t, 1), :], (8, WIDE))
        a_odd = pltpu.roll(a_even, 1, 1)
        acc = jnp.zeros((8, 128), F32)
        for s in range(PEER_ROW_WORDS):
            lhs = jnp.where(row8 == s, a_even, jnp.where(row8 == PEER_ROW_WORDS + s, a_odd, 0.0))
            acc = acc + jnp.dot(lhs.astype(BF16), _gathered_chunk(gbuf, s), preferred_element_type=F32)
        o_ref[t] = x1_ref[t] + acc

    _peer_step(rows_hbm, tab_ref, (smem_a, smem_b), sems, (gbuf_a, gbuf_b), token)


def _peer_out(rows, a_wide, tab_v, x1_3d):
    T = rows.shape[0]
    tb = 2 * PEER_CHUNK
    return pl.pallas_call(
        _peer_out_kernel,
        grid=(T // tb,),
        in_specs=[pl.BlockSpec(memory_space=pl.ANY),
                  _table_spec(tab_v),
                  pl.BlockSpec((tb, WIDE), lambda i: (i, 0)),
                  pl.BlockSpec((tb, 8, 128), lambda i: (i, 0, 0))],
        out_specs=pl.BlockSpec((tb, 8, 128), lambda i: (i, 0, 0)),
        out_shape=jax.ShapeDtypeStruct((T, 8, 128), F32),
        scratch_shapes=_peer_scratch(),
        compiler_params=_cparams(("arbitrary",)),
        name="peer_out",
    )(rows, tab_v, a_wide, x1_3d)


def _regroup_w_in(w):
    hg, att = 4 * HG_WIDTH, 3 * ATT_WIDTH
    return jnp.concatenate([w[:, hg + att:], w[:, :hg], w[:, hg:hg + att]], axis=1)


def kernel(x, norm1_g, w_in, hg_norm_g, hg_lb_logits, q_norm_g, k_norm_g, w_branch_a, w_branch_b,
           w_out, norm2_g, peer_wq, peer_subkeys, peer_u, peer_v):
    B, S, D = x.shape
    T = B * S
    assert D == D_MODEL and w_in.shape[0] == 1, "single-layer block of width 1024"
    x2d = x.reshape(T, D)

    proj = _in_proj(x2d, norm1_g[0], _regroup_w_in(w_in[0]).astype(BF16))
    y_a = _hgrn(proj, hg_lb_logits, hg_norm_g[0], B, S)
    y_b = _attn(proj, q_norm_g[0], k_norm_g[0], B, S)
    x1, h2, qp = _merge(x2d, proj, y_a, y_b, w_branch_a[0].astype(BF16), w_branch_b[0].astype(BF16),
                        w_out[0].astype(BF16), norm2_g[0], peer_wq[0].astype(BF16))
    idx_t, gate_t = _peer_topk(qp, peer_subkeys[0].astype(BF16))
    rows = idx_t.T * PEER_ROW_WORDS
    gate = gate_t.T
    gate_wide = jnp.stack([gate, jnp.zeros_like(gate)], axis=-1).reshape(T, WIDE)
    a_wide = _peer_hid(rows, _pack_table(peer_u[0]), h2.reshape(T, 8, 128), gate_wide)
    out = _peer_out(rows, a_wide, _pack_table(peer_v[0]), x1.reshape(T, 8, 128))
    return out.reshape(B, S, D)
```

```python
import functools

import jax
import jax.numpy as jnp
from jax import lax
from jax.experimental import pallas as pl
from jax.experimental.pallas import tpu as pltpu

F32 = jnp.float32
BF16 = jnp.bfloat16
I32 = jnp.int32

NORM_EPS = 1e-6
D_MODEL = 1024
HG_HEADS = 4
HG_DIM = 128
HG_CHUNK = 64
HG_WIDTH = HG_HEADS * HG_DIM
ATT_GROUPS = ((128, 1), (512, 4), (2048, 16))
ATT_BLOCK = 128
ATT_INTERLEAVE = 4
ATT_HEAD_DIM = 64
ATT_OUT_WIDTH = 256
ATT_WIDTH = 768
PEER_HEADS = 8
PEER_NKEYS = 128
PEER_HALF = 64
PEER_TOPK = 16
PEER_PAIRS = PEER_HEADS * PEER_TOPK
PEER_HALF_D = D_MODEL // 2

COL_GATE_A = 0
COL_GATE_B = D_MODEL
COL_HG = 2 * D_MODEL
COL_ATT = 2 * D_MODEL + 4 * HG_WIDTH
IN_COLS = COL_ATT + 3 * ATT_WIDTH

VMEM_LIMIT = 48 * 1024 * 1024


def _cparams(sem, vmem=VMEM_LIMIT):
    return pltpu.CompilerParams(dimension_semantics=sem, vmem_limit_bytes=vmem)


def _in_proj_kernel(x_ref, g_ref, w_ref, o_ref, h_scr):
    @pl.when(pl.program_id(1) == 0)
    def _():
        x = x_ref[...]
        ms = jnp.mean(x * x, axis=-1, keepdims=True)
        h_scr[...] = (x * lax.rsqrt(ms + NORM_EPS) * g_ref[...]).astype(BF16)

    o_ref[...] = jnp.dot(h_scr[...], w_ref[...], preferred_element_type=F32)


def _in_proj(x2d, g, w_bf16, tm=1024, tn=1280):
    T, D = x2d.shape
    N = w_bf16.shape[1]
    return pl.pallas_call(
        _in_proj_kernel,
        grid=(T // tm, N // tn),
        in_specs=[
            pl.BlockSpec((tm, D), lambda i, j: (i, 0)),
            pl.BlockSpec((1, D), lambda i, j: (0, 0)),
            pl.BlockSpec((D, tn), lambda i, j: (0, j)),
        ],
        out_specs=pl.BlockSpec((tm, tn), lambda i, j: (i, j)),
        out_shape=jax.ShapeDtypeStruct((T, N), F32),
        scratch_shapes=[pltpu.VMEM((tm, D), BF16)],
        compiler_params=_cparams(("parallel", "arbitrary")),
        name="in_proj",
    )(x2d, g.reshape(1, D), w_bf16)


def _split_bf16(x):
    hi = x.astype(BF16)
    lo = (x - hi.astype(F32)).astype(BF16)
    return hi, lo


def _hgrn_kernel(q_ref, f_ref, i_ref, og_ref, lbl_ref, ng_ref, o_ref, state_scr, *, n_chunks):
    @pl.when(pl.program_id(1) == 0)
    def _():
        state_scr[...] = jnp.zeros_like(state_scr)

    logits = lbl_ref[...]
    mx = jnp.max(logits, axis=0, keepdims=True)
    ex = jnp.exp(logits - mx)
    lb = ex[0:1, :] / jnp.sum(ex, axis=0, keepdims=True)
    ng = ng_ref[...]

    C = HG_CHUNK
    row = lax.broadcasted_iota(I32, (C, C), 0)
    col = lax.broadcasted_iota(I32, (C, C), 1)
    causal = row >= col
    tri = causal.astype(BF16)

    def chunk(c, carry):
        r0 = pl.multiple_of(c * C, C)
        rows = pl.ds(r0, C)
        f = lb + (1.0 - lb) * jax.nn.sigmoid(f_ref[rows, :])
        log_f = jnp.log(f)
        k = 1.0 - f
        q = jax.nn.sigmoid(q_ref[rows, :])
        v = i_ref[rows, :]
        og = og_ref[rows, :]
        lf_hi, lf_lo = _split_bf16(log_f)
        G = (jnp.dot(tri, lf_hi, preferred_element_type=F32)
             + jnp.dot(tri, lf_lo, preferred_element_type=F32))
        G_last = G[C - 1:C, :]
        q_dec = (q * jnp.exp(G)).astype(BF16)
        k_intra = (k * jnp.exp(-G)).astype(BF16)
        k_state = (k * jnp.exp(G_last - G)).astype(BF16)
        decay = jnp.exp(G_last)
        v_b = v.astype(BF16)
        for h in range(HG_HEADS):
            sl = slice(h * HG_DIM, (h + 1) * HG_DIM)
            st = state_scr[h]
            A = pl.dot(q_dec[:, sl], k_intra[:, sl], trans_b=True)
            A = jnp.where(causal, A, 0.0)
            o = jnp.dot(A.astype(BF16), v_b[:, sl], preferred_element_type=F32)
            o = o + pl.dot(q_dec[:, sl], st.astype(BF16), trans_b=True)
            state_scr[h] = decay[:, sl] * st + pl.dot(v_b[:, sl], k_state[:, sl], trans_a=True)
            o = o * lax.rsqrt(jnp.mean(o * o, axis=-1, keepdims=True) + NORM_EPS) * ng
            o_ref[rows, sl] = o * jax.nn.silu(og[:, sl])
        return carry

    lax.fori_loop(0, n_chunks, chunk, 0)


def _hgrn(proj, lb_logits, norm_g, B, S, ts=512):
    T = B * S
    nt = S // ts
    blk0 = COL_HG // HG_WIDTH

    def spec(kind):
        return pl.BlockSpec((ts, HG_WIDTH), lambda b, s, kind=kind: (b * nt + s, blk0 + kind))

    return pl.pallas_call(
        functools.partial(_hgrn_kernel, n_chunks=ts // HG_CHUNK),
        grid=(B, nt),
        in_specs=[spec(0), spec(1), spec(2), spec(3),
                  pl.BlockSpec(lb_logits.shape, lambda b, s: (0, 0)),
                  pl.BlockSpec((1, HG_DIM), lambda b, s: (0, 0))],
        out_specs=pl.BlockSpec((ts, HG_WIDTH), lambda b, s: (b * nt + s, 0)),
        out_shape=jax.ShapeDtypeStruct((T, HG_WIDTH), F32),
        scratch_shapes=[pltpu.VMEM((HG_HEADS, HG_DIM, HG_DIM), F32)],
        compiler_params=_cparams(("parallel", "arbitrary")),
        name="hgrn",
    )(proj, proj, proj, proj, lb_logits, norm_g.reshape(1, HG_DIM))


def _att_group(q_ref, k_ref, v_ref, gq, gk, acc_scr, m_scr, l_scr, qn_scr, kp_scr, vp_scr, *, S, d):
    L = S // d
    n_blk = L // ATT_BLOCK
    BLK = ATT_BLOCK
    lane = lax.broadcasted_iota(I32, (1, 2 * ATT_HEAD_DIM), 1)
    head_mask = [(lane < ATT_HEAD_DIM), (lane >= ATT_HEAD_DIM)]
    qi = lax.broadcasted_iota(I32, (BLK, 2 * BLK), 0)
    kj = lax.broadcasted_iota(I32, (BLK, 2 * BLK), 1)
    dist = qi - kj + BLK
    band = (dist >= 0) & (dist <= BLK)
    scale = ATT_HEAD_DIM ** -0.5

    def head_rms(t, g):
        t2 = t * t
        inv = jnp.zeros_like(t)
        for hm in head_mask:
            ms = jnp.sum(jnp.where(hm, t2, 0.0), axis=-1, keepdims=True) * (1.0 / ATT_HEAD_DIM)
            inv = jnp.where(hm, lax.rsqrt(ms + NORM_EPS), inv)
        return t * inv * g

    def prepare(r, slot):
        rows_all = pl.ds(r, L, stride=d) if d > 1 else pl.ds(0, L)
        k0 = slot * (L + BLK)
        qn_scr[pl.ds(slot * L, L), :] = head_rms(q_ref[rows_all, :], gq).astype(BF16)
        kp_scr[pl.ds(k0, BLK), :] = jnp.zeros((BLK, 128), BF16)
        vp_scr[pl.ds(k0, BLK), :] = jnp.zeros((BLK, 128), BF16)
        kp_scr[pl.ds(k0 + BLK, L), :] = head_rms(k_ref[rows_all, :], gk).astype(BF16)
        vp_scr[pl.ds(k0 + BLK, L), :] = v_ref[rows_all, :].astype(BF16)

    def block_stats(slot, n):
        b0 = n * BLK
        if not isinstance(n, int):
            b0 = pl.multiple_of(b0, BLK)
        valid = band & ((kj >= BLK) | (n > 0))
        qb = qn_scr[pl.ds(slot * L + b0, BLK), :]
        k2 = kp_scr[pl.ds(slot * (L + BLK) + b0, 2 * BLK), :]
        v2 = vp_scr[pl.ds(slot * (L + BLK) + b0, 2 * BLK), :]
        m_new = jnp.zeros((BLK, 128), F32)
        l_new = jnp.zeros((BLK, 128), F32)
        o_new = jnp.zeros((BLK, 128), F32)
        for hm in head_mask:
            qm = jnp.where(hm, qb, jnp.zeros_like(qb))
            s = pl.dot(qm, k2, trans_b=True) * scale
            s = jnp.where(valid, s, -jnp.inf)
            mh = jnp.max(s, axis=-1, keepdims=True)
            p = jnp.exp(s - mh)
            lh = jnp.sum(p, axis=-1, keepdims=True)
            oh = jnp.dot(p.astype(BF16), v2, preferred_element_type=F32)
            m_new = jnp.where(hm, mh, m_new)
            l_new = jnp.where(hm, lh, l_new)
            o_new = jnp.where(hm, oh, o_new)
        return m_new, l_new, o_new

    def fold(items):
        stats = [block_stats(slot, n) for _, slot, n in items]
        rows = [pl.ds(r + d * n * BLK, BLK, stride=d) if d > 1 else pl.ds(n * BLK, BLK)
                for r, _, n in items]
        old = [(m_scr[rw, :], l_scr[rw, :], acc_scr[rw, :]) for rw in rows]
        for rw, (m_new, l_new, o_new), (m_old, l_old, acc_old) in zip(rows, stats, old):
            m_tot = jnp.maximum(m_old, m_new)
            a_old = jnp.exp(m_old - m_tot)
            a_new = jnp.exp(m_new - m_tot)
            acc_scr[rw, :] = acc_old * a_old + o_new * a_new
            l_scr[rw, :] = l_old * a_old + l_new * a_new
            m_scr[rw, :] = m_tot

    U = ATT_INTERLEAVE
    if n_blk >= U:
        def residue(r, carry):
            prepare(r, 0)

            def blocks(i, carry2):
                fold([(r, 0, i * U + u) for u in range(U)])
                return carry2

            lax.fori_loop(0, n_blk // U, blocks, 0)
            return carry

        lax.fori_loop(0, d, residue, 0)
    else:
        per_step = U // n_blk

        def residues(i, carry):
            for u in range(per_step):
                prepare(i * per_step + u, u)
            fold([(i * per_step + u, u, n) for u in range(per_step) for n in range(n_blk)])
            return carry

        lax.fori_loop(0, d // per_step, residues, 0)


def _attn_kernel(q_ref, k_ref, v_ref, gq_ref, gk_ref, o_ref,
                 acc_scr, m_scr, l_scr, qn_scr, kp_scr, vp_scr, *, S):
    g = pl.program_id(2)

    @pl.when(g == 0)
    def _():
        acc_scr[...] = jnp.zeros_like(acc_scr)
        l_scr[...] = jnp.zeros_like(l_scr)
        m_scr[...] = jnp.full(m_scr.shape, -jnp.inf, F32)

    gq = gq_ref[...]
    gk = gk_ref[...]
    for gi, (w, d) in enumerate(ATT_GROUPS):
        assert w // d == ATT_BLOCK

        @pl.when(g == gi)
        def _(d=d):
            _att_group(q_ref, k_ref, v_ref, gq, gk, acc_scr, m_scr, l_scr,
                       qn_scr, kp_scr, vp_scr, S=S, d=d)

    @pl.when(g == len(ATT_GROUPS) - 1)
    def _():
        o_ref[...] = acc_scr[...] / l_scr[...]


def _attn(proj, q_norm_g, k_norm_g, B, S):
    T = B * S
    W = 2 * ATT_HEAD_DIM
    blk0 = COL_ATT // W
    per_kind = ATT_WIDTH // W
    per_group = ATT_OUT_WIDTH // W

    def spec(kind):
        return pl.BlockSpec((S, W), lambda b, j, g, kind=kind: (b, blk0 + kind * per_kind + g * per_group + j))

    gq2 = jnp.tile(q_norm_g.reshape(1, ATT_HEAD_DIM), (1, 2))
    gk2 = jnp.tile(k_norm_g.reshape(1, ATT_HEAD_DIM), (1, 2))
    return pl.pallas_call(
        functools.partial(_attn_kernel, S=S),
        grid=(B, per_group, len(ATT_GROUPS)),
        in_specs=[spec(0), spec(1), spec(2),
                  pl.BlockSpec((1, W), lambda b, j, g: (0, 0)),
                  pl.BlockSpec((1, W), lambda b, j, g: (0, 0))],
        out_specs=pl.BlockSpec((S, W), lambda b, j, g: (b, j)),
        out_shape=jax.ShapeDtypeStruct((T, ATT_OUT_WIDTH), F32),
        scratch_shapes=[pltpu.VMEM((S, W), F32), pltpu.VMEM((S, W), F32), pltpu.VMEM((S, W), F32),
                        pltpu.VMEM((S, W), BF16),
                        pltpu.VMEM((S + ATT_BLOCK, W), BF16),
                        pltpu.VMEM((S + ATT_BLOCK, W), BF16)],
        compiler_params=_cparams(("parallel", "parallel", "arbitrary")),
        name="attn",
    )(proj, proj, proj, gq2, gk2)


def _merge_kernel(x_ref, ga_ref, gb_ref, ya_ref, yb_ref, wa_ref, wb_ref, wo_ref, g2_ref, wq_ref,
                  x1_ref, h2_ref, qp_ref):
    pa = jnp.dot(ya_ref[...].astype(BF16), wa_ref[...], preferred_element_type=F32)
    pb = jnp.dot(yb_ref[...].astype(BF16), wb_ref[...], preferred_element_type=F32)
    merged = jax.nn.sigmoid(ga_ref[...]) * pa + jax.nn.sigmoid(gb_ref[...]) * pb
    x1 = x_ref[...] + jnp.dot(merged.astype(BF16), wo_ref[...], preferred_element_type=F32)
    x1_ref[...] = x1
    ms = jnp.mean(x1 * x1, axis=-1, keepdims=True)
    h2 = x1 * lax.rsqrt(ms + NORM_EPS) * g2_ref[...]
    h2_ref[...] = h2
    qp_ref[...] = jnp.dot(h2.astype(BF16), wq_ref[...], preferred_element_type=F32)


def _merge(x2d, proj, y_a, y_b, wa, wb, wo, g2, wq, tm=256):
    T, D = x2d.shape
    row = lambda i: (i, 0)
    const = lambda i: (0, 0)
    out_sd = jax.ShapeDtypeStruct((T, D), F32)
    return pl.pallas_call(
        _merge_kernel,
        grid=(T // tm,),
        in_specs=[
            pl.BlockSpec((tm, D), row),
            pl.BlockSpec((tm, D), lambda i: (i, COL_GATE_A // D_MODEL)),
            pl.BlockSpec((tm, D), lambda i: (i, COL_GATE_B // D_MODEL)),
            pl.BlockSpec((tm, HG_WIDTH), row),
            pl.BlockSpec((tm, ATT_OUT_WIDTH), row),
            pl.BlockSpec(wa.shape, const),
            pl.BlockSpec(wb.shape, const),
            pl.BlockSpec(wo.shape, const),
            pl.BlockSpec((1, D), const),
            pl.BlockSpec(wq.shape, const),
        ],
        out_specs=[pl.BlockSpec((tm, D), row)] * 3,
        out_shape=[out_sd, out_sd, out_sd],
        compiler_params=_cparams(("parallel",)),
        name="merge",
    )(x2d, proj, proj, y_a, y_b, wa, wb, wo, g2.reshape(1, D), wq)


def _top16_cols(s, code):
    big = jnp.int32(2 ** 30)
    vals, codes = [], []
    for _ in range(PEER_TOPK):
        m = jnp.max(s, axis=0, keepdims=True)
        c = jnp.min(jnp.where(s == m, code, big), axis=0, keepdims=True)
        s = jnp.where(code == c, -jnp.inf, s)
        vals.append(m)
        codes.append(c)
    return vals, codes


_CAND = [(a, b) for a in range(PEER_TOPK) for b in range(PEER_TOPK) if (a + 1) * (b + 1) <= PEER_TOPK]
_CAND_ROWS = -(-len(_CAND) // 8) * 8


def _peer_topk_kernel(qp_ref, keys_ref, code_ref, idx_ref, gate_ref):
    n = qp_ref.shape[0]
    key_code = lax.broadcasted_iota(I32, (PEER_NKEYS, 1), 0)
    cand_code = code_ref[...]
    pad =[jnp.full((1, n), -jnp.inf, F32)] * (_CAND_ROWS - len(_CAND))
    pad_i = [jnp.zeros((1, n), I32)] * (_CAND_ROWS - len(_CAND))
    for h in range(PEER_HEADS):
        sel = []
        for p in range(2):
            q = qp_ref[:, pl.ds((2 * h + p) * PEER_HALF, PEER_HALF)].astype(BF16)
            s = pl.dot(keys_ref[h, p], q, trans_b=True)
            sel.append(_top16_cols(s, key_code))
        (v1, i1), (v2, i2) = sel
        cand_s = jnp.concatenate([v1[a] + v2[b] for a, b in _CAND] + pad, axis=0)
        cand_i = jnp.concatenate([i1[a] * PEER_NKEYS + i2[b] for a, b in _CAND] + pad_i, axis=0)
        top_s, top_c = _top16_cols(cand_s, cand_code)
        ids = [jnp.sum(jnp.where(cand_code == c, cand_i, 0), axis=0, keepdims=True) for c in top_c]
        ts = jnp.concatenate(top_s, axis=0)
        e = jnp.exp(ts - ts[0:1, :])
        gate = e / jnp.sum(e, axis=0, keepdims=True)
        rows = pl.ds(h * PEER_TOPK, PEER_TOPK)
        idx_ref[rows, :] = jnp.concatenate(ids, axis=0)
        gate_ref[rows, :] = gate


def _peer_topk(qp, keys_bf16, tm=256):
    T, D = qp.shape
    codes = jnp.array([a * PEER_TOPK + b for a, b in _CAND]
                      + [2 ** 20 + i for i in range(_CAND_ROWS - len(_CAND))], I32).reshape(_CAND_ROWS, 1)
    return pl.pallas_call(
        _peer_topk_kernel,
        grid=(T // tm,),
        in_specs=[pl.BlockSpec((tm, D), lambda i: (i, 0)),
                  pl.BlockSpec(keys_bf16.shape, lambda i: (0, 0, 0, 0)),
                  pl.BlockSpec((_CAND_ROWS, 1), lambda i: (0, 0))],
        out_specs=[pl.BlockSpec((PEER_PAIRS, tm), lambda i: (0, i))] * 2,
        out_shape=[jax.ShapeDtypeStruct((PEER_PAIRS, T), I32),
                   jax.ShapeDtypeStruct((PEER_PAIRS, T), F32)],
        compiler_params=_cparams(("parallel",)),
        name="peer_topk",
    )(qp, keys_bf16, codes)


PEER_ROW_WORDS = 4
GATHER_ROWS = PEER_PAIRS * PEER_ROW_WORDS
WIDE = 2 * PEER_PAIRS


def _pack_table(w):
    n = w.shape[0]

    def bf16_bits(part):
        return lax.bitcast_convert_type(part.astype(BF16).astype(F32), jnp.uint32)

    packed = (bf16_bits(w[:, :PEER_HALF_D]) >> 16) | bf16_bits(w[:, PEER_HALF_D:])
    return lax.bitcast_convert_type(packed, I32).reshape(n * PEER_ROW_WORDS, 128)


def _table_spec(tab):
    return pl.BlockSpec(tab.shape, lambda i: (0, 0), pipeline_mode=pl.Buffered(1))


def _gather_rows(row_smem, t, tab_ref, gbuf):
    for m in range(PEER_PAIRS // 2):
        word = row_smem[t, m]
        for half, r in enumerate((word & 0xFFFF, lax.shift_right_logical(word, 16))):
            j = 2 * m + half
            r = pl.multiple_of(r, PEER_ROW_WORDS)
            gbuf[pl.ds(PEER_ROW_WORDS * j, PEER_ROW_WORDS), :] = tab_ref[pl.ds(r, PEER_ROW_WORDS), :]


def _pipelined_tokens(tb, row_smem, tab_ref, gbuf_a, gbuf_b, compute):
    _gather_rows(row_smem, 0, tab_ref, gbuf_a)
    _gather_rows(row_smem, 1, tab_ref, gbuf_b)

    def two_tokens(i, carry):
        t = 2 * i
        compute(t, gbuf_a)
        _gather_rows(row_smem, jnp.minimum(t + 2, tb - 1), tab_ref, gbuf_a)
        compute(t + 1, gbuf_b)
        _gather_rows(row_smem, jnp.minimum(t + 3, tb - 1), tab_ref, gbuf_b)
        return carry

    lax.fori_loop(0, tb // 2, two_tokens, 0)


def _gathered_chunk(gbuf, s):
    return pltpu.bitcast(gbuf[pl.ds(s, PEER_PAIRS, stride=PEER_ROW_WORDS), :], BF16)


def _gelu_exact(x):
    return 0.5 * x * (1.0 + lax.erf(x * (2.0 ** -0.5)))


def _peer_hid_kernel(row_smem, tab_ref, x_ref, gate_ref, a_ref, gbuf_a, gbuf_b, raw_scr, *, tb):
    def token(t, gbuf):
        acc = jnp.zeros((8, WIDE), F32)
        for s in range(PEER_ROW_WORDS):
            xs = x_ref[t, pl.ds(s, 2, stride=PEER_ROW_WORDS), :]
            xs_hi = xs.astype(BF16).astype(F32)
            lhs = jnp.concatenate([xs_hi, xs - xs_hi, jnp.zeros((4, 128), F32)], axis=0).astype(BF16)
            acc = acc + pl.dot(lhs, _gathered_chunk(gbuf, s), trans_b=True)
        raw_scr[t] = acc

    _pipelined_tokens(tb, row_smem, tab_ref, gbuf_a, gbuf_b, token)
    lo = raw_scr[:, 0, :] + raw_scr[:, 2, :]
    hi = raw_scr[:, 1, :] + raw_scr[:, 3, :]
    hid = lo + pltpu.roll(hi, WIDE - 1, 1)
    a_ref[...] = _gelu_exact(hid) * gate_ref[...]


def _peer_hid(rows, tab_u, x3d, gate_wide, tb=128):
    T = rows.shape[0]
    return pl.pallas_call(
        functools.partial(_peer_hid_kernel, tb=tb),
        grid=(T // tb,),
        in_specs=[pl.BlockSpec((tb, PEER_PAIRS // 2), lambda i: (i, 0), memory_space=pltpu.SMEM),
                  _table_spec(tab_u),
                  pl.BlockSpec((tb, 8, 128), lambda i: (i, 0, 0)),
                  pl.BlockSpec((tb, WIDE), lambda i: (i, 0))],
        out_specs=pl.BlockSpec((tb, WIDE), lambda i: (i, 0)),
        out_shape=jax.ShapeDtypeStruct((T, WIDE), F32),
        scratch_shapes=[pltpu.VMEM((GATHER_ROWS, 128), I32),
                        pltpu.VMEM((GATHER_ROWS, 128), I32),
                        pltpu.VMEM((tb, 8, WIDE), F32)],
        compiler_params=_cparams(("arbitrary",)),
        name="peer_hid",
    )(rows, tab_u, x3d, gate_wide)


def _peer_out_kernel(row_smem, tab_ref, a_ref, x1_ref, o_ref, gbuf_a, gbuf_b, *, tb):
    row8 = lax.broadcasted_iota(I32, (8, WIDE), 0)

    def token(t, gbuf):
        a_even = jnp.broadcast_to(a_ref[pl.ds(t, 1), :], (8, WIDE))
        a_odd = pltpu.roll(a_even, 1, 1)
        acc = jnp.zeros((8, 128), F32)
        for s in range(PEER_ROW_WORDS):
            lhs = jnp.where(row8 == s, a_even, jnp.where(row8 == PEER_ROW_WORDS + s, a_odd, 0.0))
            acc = acc + jnp.dot(lhs.astype(BF16), _gathered_chunk(gbuf, s), preferred_element_type=F32)
        o_ref[t] = x1_ref[t] + acc

    _pipelined_tokens(tb, row_smem, tab_ref, gbuf_a, gbuf_b, token)


def _peer_out(rows, a_wide, tab_v, x1_3d, tb=128):
    T = rows.shape[0]
    return pl.pallas_call(
        functools.partial(_peer_out_kernel, tb=tb),
        grid=(T // tb,),
        in_specs=[pl.BlockSpec((tb, PEER_PAIRS // 2), lambda i: (i, 0), memory_space=pltpu.SMEM),
                  _table_spec(tab_v),
                  pl.BlockSpec((tb, WIDE), lambda i: (i, 0)),
                  pl.BlockSpec((tb, 8, 128), lambda i: (i, 0, 0))],
        out_specs=pl.BlockSpec((tb, 8, 128), lambda i: (i, 0, 0)),
        out_shape=jax.ShapeDtypeStruct((T, 8, 128), F32),
        scratch_shapes=[pltpu.VMEM((GATHER_ROWS, 128), I32),
                        pltpu.VMEM((GATHER_ROWS, 128), I32)],
        compiler_params=_cparams(("arbitrary",)),
        name="peer_out",
    )(rows, tab_v, a_wide, x1_3d)


def _regroup_w_in(w):
    hg, att = 4 * HG_WIDTH, 3 * ATT_WIDTH
    return jnp.concatenate([w[:, hg + att:], w[:, :hg], w[:, hg:hg + att]], axis=1)


def kernel(x, norm1_g, w_in, hg_norm_g, hg_lb_logits, q_norm_g, k_norm_g, w_branch_a, w_branch_b,
           w_out, norm2_g, peer_wq, peer_subkeys, peer_u, peer_v):
    B, S, D = x.shape
    T = B * S
    assert D == D_MODEL and w_in.shape[0] == 1, "single-layer block of width 1024"
    x2d = x.reshape(T, D)

    proj = _in_proj(x2d, norm1_g[0], _regroup_w_in(w_in[0]).astype(BF16))
    y_a = _hgrn(proj, hg_lb_logits, hg_norm_g[0], B, S)
    y_b = _attn(proj, q_norm_g[0], k_norm_g[0], B, S)
    x1, h2, qp = _merge(x2d, proj, y_a, y_b, w_branch_a[0].astype(BF16), w_branch_b[0].astype(BF16),
                        w_out[0].astype(BF16), norm2_g[0], peer_wq[0].astype(BF16))
    idx_t, gate_t = _peer_topk(qp, peer_subkeys[0].astype(BF16))
    rows = idx_t.T * PEER_ROW_WORDS
    rows = rows[:, 0::2] | (rows[:, 1::2] << 16)
    gate = gate_t.T
    gate_wide = jnp.stack([gate, jnp.zeros_like(gate)], axis=-1).reshape(T, WIDE)
    a_wide = _peer_hid(rows, _pack_table(peer_u[0]), h2.reshape(T, 8, 128), gate_wide)
    out = _peer_out(rows, a_wide, _pack_table(peer_v[0]), x1.reshape(T, 8, 128))
    return out.reshape(B, S, D)
```

```python
import functools

import jax
import jax.numpy as jnp
from jax import lax
from jax.experimental import pallas as pl
from jax.experimental.pallas import tpu as pltpu

F32 = jnp.float32
BF16 = jnp.bfloat16
I32 = jnp.int32

NORM_EPS = 1e-6
D_MODEL = 1024
HG_HEADS = 4
HG_DIM = 128
HG_CHUNK = 64
HG_WIDTH = HG_HEADS * HG_DIM
ATT_GROUPS = ((128, 1), (512, 4), (2048, 16))
ATT_BLOCK = 128
ATT_INTERLEAVE = 4
ATT_HEAD_DIM = 64
ATT_OUT_WIDTH = 256
ATT_WIDTH = 768
PEER_HEADS = 8
PEER_NKEYS = 128
PEER_HALF = 64
PEER_TOPK = 16
PEER_PAIRS = PEER_HEADS * PEER_TOPK
PEER_HALF_D = D_MODEL // 2

COL_GATE_A = 0
COL_GATE_B = D_MODEL
COL_HG = 2 * D_MODEL
COL_ATT = 2 * D_MODEL + 4 * HG_WIDTH
IN_COLS = COL_ATT + 3 * ATT_WIDTH

VMEM_LIMIT = 48 * 1024 * 1024


def _cparams(sem, vmem=VMEM_LIMIT):
    return pltpu.CompilerParams(dimension_semantics=sem, vmem_limit_bytes=vmem)


def _in_proj_kernel(x_ref, g_ref, w_ref, o_ref, h_scr):
    @pl.when(pl.program_id(1) == 0)
    def _():
        x = x_ref[...]
        ms = jnp.mean(x * x, axis=-1, keepdims=True)
        h_scr[...] = (x * lax.rsqrt(ms + NORM_EPS) * g_ref[...]).astype(BF16)

    o_ref[...] = jnp.dot(h_scr[...], w_ref[...], preferred_element_type=F32)


def _in_proj(x2d, g, w_bf16, tm=1024, tn=1280):
    T, D = x2d.shape
    N = w_bf16.shape[1]
    return pl.pallas_call(
        _in_proj_kernel,
        grid=(T // tm, N // tn),
        in_specs=[
            pl.BlockSpec((tm, D), lambda i, j: (i, 0)),
            pl.BlockSpec((1, D), lambda i, j: (0, 0)),
            pl.BlockSpec((D, tn), lambda i, j: (0, j)),
        ],
        out_specs=pl.BlockSpec((tm, tn), lambda i, j: (i, j)),
        out_shape=jax.ShapeDtypeStruct((T, N), F32),
        scratch_shapes=[pltpu.VMEM((tm, D), BF16)],
        compiler_params=_cparams(("parallel", "arbitrary")),
        name="in_proj",
    )(x2d, g.reshape(1, D), w_bf16)


def _split_bf16(x):
    hi = x.astype(BF16)
    lo = (x - hi.astype(F32)).astype(BF16)
    return hi, lo


def _hgrn_kernel(q_ref, f_ref, i_ref, og_ref, lbl_ref, ng_ref, o_ref, state_scr, *, n_chunks):
    @pl.when(pl.program_id(1) == 0)
    def _():
        state_scr[...] = jnp.zeros_like(state_scr)

    logits = lbl_ref[...]
    mx = jnp.max(logits, axis=0, keepdims=True)
    ex = jnp.exp(logits - mx)
    lb = ex[0:1, :] / jnp.sum(ex, axis=0, keepdims=True)
    ng = ng_ref[...]

    C = HG_CHUNK
    row = lax.broadcasted_iota(I32, (C, C), 0)
    col = lax.broadcasted_iota(I32, (C, C), 1)
    causal = row >= col
    tri = causal.astype(BF16)

    def chunk(c, carry):
        r0 = pl.multiple_of(c * C, C)
        rows = pl.ds(r0, C)
        f = lb + (1.0 - lb) * jax.nn.sigmoid(f_ref[rows, :])
        log_f = jnp.log(f)
        k = 1.0 - f
        q = jax.nn.sigmoid(q_ref[rows, :])
        v = i_ref[rows, :]
        og = og_ref[rows, :]
        lf_hi, lf_lo = _split_bf16(log_f)
        G = (jnp.dot(tri, lf_hi, preferred_element_type=F32)
             + jnp.dot(tri, lf_lo, preferred_element_type=F32))
        G_last = G[C - 1:C, :]
        q_dec = (q * jnp.exp(G)).astype(BF16)
        k_intra = (k * jnp.exp(-G)).astype(BF16)
        k_state = (k * jnp.exp(G_last - G)).astype(BF16)
        decay = jnp.exp(G_last)
        v_b = v.astype(BF16)
        for h in range(HG_HEADS):
            sl = slice(h * HG_DIM, (h + 1) * HG_DIM)
            st = state_scr[h]
            A = pl.dot(q_dec[:, sl], k_intra[:, sl], trans_b=True)
            A = jnp.where(causal, A, 0.0)
            o = jnp.dot(A.astype(BF16), v_b[:, sl], preferred_element_type=F32)
            o = o + pl.dot(q_dec[:, sl], st.astype(BF16), trans_b=True)
            state_scr[h] = decay[:, sl] * st + pl.dot(v_b[:, sl], k_state[:, sl], trans_a=True)
            o = o * lax.rsqrt(jnp.mean(o * o, axis=-1, keepdims=True) + NORM_EPS) * ng
            o_ref[rows, sl] = o * jax.nn.silu(og[:, sl])
        return carry

    lax.fori_loop(0, n_chunks, chunk, 0)


def _hgrn(proj, lb_logits, norm_g, B, S, ts=512):
    T = B * S
    nt = S // ts
    blk0 = COL_HG // HG_WIDTH

    def spec(kind):
        return pl.BlockSpec((ts, HG_WIDTH), lambda b, s, kind=kind: (b * nt + s, blk0 + kind))

    return pl.pallas_call(
        functools.partial(_hgrn_kernel, n_chunks=ts // HG_CHUNK),
        grid=(B, nt),
        in_specs=[spec(0), spec(1), spec(2), spec(3),
                  pl.BlockSpec(lb_logits.shape, lambda b, s: (0, 0)),
                  pl.BlockSpec((1, HG_DIM), lambda b, s: (0, 0))],
        out_specs=pl.BlockSpec((ts, HG_WIDTH), lambda b, s: (b * nt + s, 0)),
        out_shape=jax.ShapeDtypeStruct((T, HG_WIDTH), F32),
        scratch_shapes=[pltpu.VMEM((HG_HEADS, HG_DIM, HG_DIM), F32)],
        compiler_params=_cparams(("parallel", "arbitrary")),
        name="hgrn",
    )(proj, proj, proj, proj, lb_logits, norm_g.reshape(1, HG_DIM))


def _att_group(q_ref, k_ref, v_ref, gq, gk, acc_scr, m_scr, l_scr, qn_scr, kp_scr, vp_scr, *, S, d):
    L = S // d
    n_blk = L // ATT_BLOCK
    BLK = ATT_BLOCK
    lane = lax.broadcasted_iota(I32, (1, 2 * ATT_HEAD_DIM), 1)
    head_mask = [(lane < ATT_HEAD_DIM), (lane >= ATT_HEAD_DIM)]
    qi = lax.broadcasted_iota(I32, (BLK, 2 * BLK), 0)
    kj = lax.broadcasted_iota(I32, (BLK, 2 * BLK), 1)
    dist = qi - kj + BLK
    band = (dist >= 0) & (dist <= BLK)
    scale = ATT_HEAD_DIM ** -0.5

    def head_rms(t, g):
        t2 = t * t
        inv = jnp.zeros_like(t)
        for hm in head_mask:
            ms = jnp.sum(jnp.where(hm, t2, 0.0), axis=-1, keepdims=True) * (1.0 / ATT_HEAD_DIM)
            inv = jnp.where(hm, lax.rsqrt(ms + NORM_EPS), inv)
        return t * inv * g

    def prepare(r, slot):
        rows_all = pl.ds(r, L, stride=d) if d > 1 else pl.ds(0, L)
        k0 = slot * (L + BLK)
        qn_scr[pl.ds(slot * L, L), :] = head_rms(q_ref[rows_all, :], gq).astype(BF16)
        kp_scr[pl.ds(k0, BLK), :] = jnp.zeros((BLK, 128), BF16)
        vp_scr[pl.ds(k0, BLK), :] = jnp.zeros((BLK, 128), BF16)
        kp_scr[pl.ds(k0 + BLK, L), :] = head_rms(k_ref[rows_all, :], gk).astype(BF16)
        vp_scr[pl.ds(k0 + BLK, L), :] = v_ref[rows_all, :].astype(BF16)

    def block_stats(slot, n):
        b0 = n * BLK
        if not isinstance(n, int):
            b0 = pl.multiple_of(b0, BLK)
        valid = band & ((kj >= BLK) | (n > 0))
        qb = qn_scr[pl.ds(slot * L + b0, BLK), :]
        k2 = kp_scr[pl.ds(slot * (L + BLK) + b0, 2 * BLK), :]
        v2 = vp_scr[pl.ds(slot * (L + BLK) + b0, 2 * BLK), :]
        m_new = jnp.zeros((BLK, 128), F32)
        l_new = jnp.zeros((BLK, 128), F32)
        o_new = jnp.zeros((BLK, 128), F32)
        for hm in head_mask:
            qm = jnp.where(hm, qb, jnp.zeros_like(qb))
            s = pl.dot(qm, k2, trans_b=True) * scale
            s = jnp.where(valid, s, -jnp.inf)
            mh = jnp.max(s, axis=-1, keepdims=True)
            p = jnp.exp(s - mh)
            lh = jnp.sum(p, axis=-1, keepdims=True)
            oh = jnp.dot(p.astype(BF16), v2, preferred_element_type=F32)
            m_new = jnp.where(hm, mh, m_new)
            l_new = jnp.where(hm, lh, l_new)
            o_new = jnp.where(hm, oh, o_new)
        return m_new, l_new, o_new

    def fold(items):
        stats = [block_stats(slot, n) for _, slot, n in items]
        rows = [pl.ds(r + d * n * BLK, BLK, stride=d) if d > 1 else pl.ds(n * BLK, BLK)
                for r, _, n in items]
        old = [(m_scr[rw, :], l_scr[rw, :], acc_scr[rw, :]) for rw in rows]
        for rw, (m_new, l_new, o_new), (m_old, l_old, acc_old) in zip(rows, stats, old):
            m_tot = jnp.maximum(m_old, m_new)
            a_old = jnp.exp(m_old - m_tot)
            a_new = jnp.exp(m_new - m_tot)
            acc_scr[rw, :] = acc_old * a_old + o_new * a_new
            l_scr[rw, :] = l_old * a_old + l_new * a_new
            m_scr[rw, :] = m_tot

    U = ATT_INTERLEAVE
    if n_blk >= U:
        def residue(r, carry):
            prepare(r, 0)

            def blocks(i, carry2):
                fold([(r, 0, i * U + u) for u in range(U)])
                return carry2

            lax.fori_loop(0, n_blk // U, blocks, 0)
            return carry

        lax.fori_loop(0, d, residue, 0)
    else:
        per_step = U // n_blk

        def residues(i, carry):
            for u in range(per_step):
                prepare(i * per_step + u, u)
            fold([(i * per_step + u, u, n) for u in range(per_step) for n in range(n_blk)])
            return carry

        lax.fori_loop(0, d // per_step, residues, 0)


def _attn_kernel(q_ref, k_ref, v_ref, gq_ref, gk_ref, o_ref,
                 acc_scr, m_scr, l_scr, qn_scr, kp_scr, vp_scr, *, S):
    g = pl.program_id(2)

    @pl.when(g == 0)
    def _():
        acc_scr[...] = jnp.zeros_like(acc_scr)
        l_scr[...] = jnp.zeros_like(l_scr)
        m_scr[...] = jnp.full(m_scr.shape, -jnp.inf, F32)

    gq = gq_ref[...]
    gk = gk_ref[...]
    for gi, (w, d) in enumerate(ATT_GROUPS):
        assert w // d == ATT_BLOCK

        @pl.when(g == gi)
        def _(d=d):
            _att_group(q_ref, k_ref, v_ref, gq, gk, acc_scr, m_scr, l_scr,
                       qn_scr, kp_scr, vp_scr, S=S, d=d)

    @pl.when(g == len(ATT_GROUPS) - 1)
    def _():
        o_ref[...] = acc_scr[...] / l_scr[...]


def _attn(proj, q_norm_g, k_norm_g, B, S):
    T = B * S
    W = 2 * ATT_HEAD_DIM
    blk0 = COL_ATT // W
    per_kind = ATT_WIDTH // W
    per_group = ATT_OUT_WIDTH // W

    def spec(kind):
        return pl.BlockSpec((S, W), lambda b, j, g, kind=kind: (b, blk0 + kind * per_kind + g * per_group + j))

    gq2 = jnp.tile(q_norm_g.reshape(1, ATT_HEAD_DIM), (1, 2))
    gk2 = jnp.tile(k_norm_g.reshape(1, ATT_HEAD_DIM), (1, 2))
    return pl.pallas_call(
        functools.partial(_attn_kernel, S=S),
        grid=(B, per_group, len(ATT_GROUPS)),
        in_specs=[spec(0), spec(1), spec(2),
                  pl.BlockSpec((1, W), lambda b, j, g: (0, 0)),
                  pl.BlockSpec((1, W), lambda b, j, g: (0, 0))],
        out_specs=pl.BlockSpec((S, W), lambda b, j, g: (b, j)),
        out_shape=jax.ShapeDtypeStruct((T, ATT_OUT_WIDTH), F32),
        scratch_shapes=[pltpu.VMEM((S, W), F32), pltpu.VMEM((S, W), F32), pltpu.VMEM((S, W), F32),
                        pltpu.VMEM((S, W), BF16),
                        pltpu.VMEM((S + ATT_BLOCK, W), BF16),
                        pltpu.VMEM((S + ATT_BLOCK, W), BF16)],
        compiler_params=_cparams(("parallel", "parallel", "arbitrary")),
        name="attn",
    )(proj, proj, proj, gq2, gk2)


def _merge_kernel(x_ref, ga_ref, gb_ref, ya_ref, yb_ref, wa_ref, wb_ref, wo_ref, g2_ref, wq_ref,
                  x1_ref, h2_ref, qp_ref):
    pa = jnp.dot(ya_ref[...].astype(BF16), wa_ref[...], preferred_element_type=F32)
    pb = jnp.dot(yb_ref[...].astype(BF16), wb_ref[...], preferred_element_type=F32)
    merged = jax.nn.sigmoid(ga_ref[...]) * pa + jax.nn.sigmoid(gb_ref[...]) * pb
    x1 = x_ref[...] + jnp.dot(merged.astype(BF16), wo_ref[...], preferred_element_type=F32)
    x1_ref[...] = x1
    ms = jnp.mean(x1 * x1, axis=-1, keepdims=True)
    h2 = x1 * lax.rsqrt(ms + NORM_EPS) * g2_ref[...]
    h2_ref[...] = h2
    qp_ref[...] = jnp.dot(h2.astype(BF16), wq_ref[...], preferred_element_type=F32)


def _merge(x2d, proj, y_a, y_b, wa, wb, wo, g2, wq, tm=256):
    T, D = x2d.shape
    row = lambda i: (i, 0)
    const = lambda i: (0, 0)
    out_sd = jax.ShapeDtypeStruct((T, D), F32)
    return pl.pallas_call(
        _merge_kernel,
        grid=(T // tm,),
        in_specs=[
            pl.BlockSpec((tm, D), row),
            pl.BlockSpec((tm, D), lambda i: (i, COL_GATE_A // D_MODEL)),
            pl.BlockSpec((tm, D), lambda i: (i, COL_GATE_B // D_MODEL)),
            pl.BlockSpec((tm, HG_WIDTH), row),
            pl.BlockSpec((tm, ATT_OUT_WIDTH), row),
            pl.BlockSpec(wa.shape, const),
            pl.BlockSpec(wb.shape, const),
            pl.BlockSpec(wo.shape, const),
            pl.BlockSpec((1, D), const),
            pl.BlockSpec(wq.shape, const),
        ],
        out_specs=[pl.BlockSpec((tm, D), row)] * 3,
        out_shape=[out_sd, out_sd, out_sd],
        compiler_params=_cparams(("parallel",)),
        name="merge",
    )(x2d, proj, proj, y_a, y_b, wa, wb, wo, g2.reshape(1, D), wq)


def _top16_cols(s, code):
    big = jnp.float32(2 ** 30)
    vals, codes = [], []
    for _ in range(PEER_TOPK):
        m = jnp.max(s, axis=0, keepdims=True)
        c = jnp.min(jnp.where(s == m, code, big), axis=0, keepdims=True)
        s = jnp.where(code == c, -jnp.inf, s)
        vals.append(m)
        codes.append(c)
    return vals, codes


_CAND = [(a, b) for a in range(PEER_TOPK) for b in range(PEER_TOPK) if (a + 1) * (b + 1) <= PEER_TOPK]
_CAND_ROWS = -(-len(_CAND) // 8) * 8


def _peer_topk_kernel(qp_ref, keys_ref, code_ref, idx_ref, gate_ref):
    n = qp_ref.shape[0]
    key_code = lax.broadcasted_iota(I32, (PEER_NKEYS, 1), 0).astype(F32)
    cand_code = code_ref[...]
    pad = [jnp.full((1, n), -jnp.inf, F32)] * (_CAND_ROWS - len(_CAND))
    pad_i = [jnp.zeros((1, n), F32)] * (_CAND_ROWS - len(_CAND))
    for h in range(PEER_HEADS):
        sel = []
        for p in range(2):
            q = qp_ref[:, pl.ds((2 * h + p) * PEER_HALF, PEER_HALF)].astype(BF16)
            s = pl.dot(keys_ref[h, p], q, trans_b=True)
            sel.append(_top16_cols(s, key_code))
        (v1, i1), (v2, i2) = sel
        cand_s = jnp.concatenate([v1[a] + v2[b] for a, b in _CAND] + pad, axis=0)
        cand_i = jnp.concatenate([i1[a] * PEER_NKEYS + i2[b] for a, b in _CAND] + pad_i, axis=0)
        top_s, top_c = _top16_cols(cand_s, cand_code)
        ids = [jnp.sum(jnp.where(cand_code == c, cand_i, 0.0), axis=0, keepdims=True) for c in top_c]
        ts = jnp.concatenate(top_s, axis=0)
        e = jnp.exp(ts - ts[0:1, :])
        gate = e / jnp.sum(e, axis=0, keepdims=True)
        rows = pl.ds(h * PEER_TOPK, PEER_TOPK)
        idx_ref[rows, :] = jnp.concatenate(ids, axis=0).astype(I32)
        gate_ref[rows, :] = gate


def _peer_topk(qp, keys_bf16, tm=256):
    T, D = qp.shape
    codes = jnp.array([a * PEER_TOPK + b for a, b in _CAND]
                      + [2 ** 20 + i for i in range(_CAND_ROWS - len(_CAND))], F32).reshape(_CAND_ROWS, 1)
    return pl.pallas_call(
        _peer_topk_kernel,
        grid=(T // tm,),
        in_specs=[pl.BlockSpec((tm, D), lambda i: (i, 0)),
                  pl.BlockSpec(keys_bf16.shape, lambda i: (0, 0, 0, 0)),
                  pl.BlockSpec((_CAND_ROWS, 1), lambda i: (0, 0))],
        out_specs=[pl.BlockSpec((PEER_PAIRS, tm), lambda i: (0, i))] * 2,
        out_shape=[jax.ShapeDtypeStruct((PEER_PAIRS, T), I32),
                   jax.ShapeDtypeStruct((PEER_PAIRS, T), F32)],
        compiler_params=_cparams(("parallel",)),
        name="peer_topk",
    )(qp, keys_bf16, codes)


PEER_ROW_WORDS = 4
GATHER_ROWS = PEER_PAIRS * PEER_ROW_WORDS
WIDE = 2 * PEER_PAIRS


def _pack_table(w):
    n = w.shape[0]

    def bf16_bits(part):
        return lax.bitcast_convert_type(part.astype(BF16).astype(F32), jnp.uint32)

    packed = (bf16_bits(w[:, :PEER_HALF_D]) >> 16) | bf16_bits(w[:, PEER_HALF_D:])
    return lax.bitcast_convert_type(packed, I32).reshape(n * PEER_ROW_WORDS, 128)


def _table_spec(tab):
    return pl.BlockSpec(tab.shape, lambda i: (0, 0), pipeline_mode=pl.Buffered(1))


def _gather_rows(row_smem, t, tab_ref, gbuf):
    for j in range(PEER_PAIRS):
        r = pl.multiple_of(row_smem[t, j], PEER_ROW_WORDS)
        gbuf[pl.ds(PEER_ROW_WORDS * j, PEER_ROW_WORDS), :] = tab_ref[pl.ds(r, PEER_ROW_WORDS), :]


def _pipelined_tokens(tb, row_smem, tab_ref, gbuf_a, gbuf_b, compute):
    _gather_rows(row_smem, 0, tab_ref, gbuf_a)
    _gather_rows(row_smem, 1, tab_ref, gbuf_b)

    def two_tokens(i, carry):
        t = 2 * i
        compute(t, gbuf_a)
        _gather_rows(row_smem, jnp.minimum(t + 2, tb - 1), tab_ref, gbuf_a)
        compute(t + 1, gbuf_b)
        _gather_rows(row_smem, jnp.minimum(t + 3, tb - 1), tab_ref, gbuf_b)
        return carry

    lax.fori_loop(0, tb // 2, two_tokens, 0)


def _gathered_chunk(gbuf, s):
    return pltpu.bitcast(gbuf[pl.ds(s, PEER_PAIRS, stride=PEER_ROW_WORDS), :], BF16)


def _gelu_exact(x):
    return 0.5 * x * (1.0 + lax.erf(x * (2.0 ** -0.5)))


def _peer_hid_kernel(row_smem, tab_ref, x_ref, gate_ref, a_ref, gbuf_a, gbuf_b, raw_scr, *, tb):
    def token(t, gbuf):
        acc = jnp.zeros((8, WIDE), F32)
        for s in range(PEER_ROW_WORDS):
            xs = x_ref[t, pl.ds(s, 2, stride=PEER_ROW_WORDS), :]
            xs_hi = xs.astype(BF16).astype(F32)
            lhs = jnp.concatenate([xs_hi, xs - xs_hi, jnp.zeros((4, 128), F32)], axis=0).astype(BF16)
            acc = acc + pl.dot(lhs, _gathered_chunk(gbuf, s), trans_b=True)
        raw_scr[t] = acc

    _pipelined_tokens(tb, row_smem, tab_ref, gbuf_a, gbuf_b, token)
    lo = raw_scr[:, 0, :] + raw_scr[:, 2, :]
    hi = raw_scr[:, 1, :] + raw_scr[:, 3, :]
    hid = lo + pltpu.roll(hi, WIDE - 1, 1)
    a_ref[...] = _gelu_exact(hid) * gate_ref[...]


def _peer_hid(rows, tab_u, x3d, gate_wide, tb=128):
    T = rows.shape[0]
    return pl.pallas_call(
        functools.partial(_peer_hid_kernel, tb=tb),
        grid=(T // tb,),
        in_specs=[pl.BlockSpec((tb, PEER_PAIRS), lambda i: (i, 0), memory_space=pltpu.SMEM),
                  _table_spec(tab_u),
                  pl.BlockSpec((tb, 8, 128), lambda i: (i, 0, 0)),
                  pl.BlockSpec((tb, WIDE), lambda i: (i, 0))],
        out_specs=pl.BlockSpec((tb, WIDE), lambda i: (i, 0)),
        out_shape=jax.ShapeDtypeStruct((T, WIDE), F32),
        scratch_shapes=[pltpu.VMEM((GATHER_ROWS, 128), I32),
                        pltpu.VMEM((GATHER_ROWS, 128), I32),
                        pltpu.VMEM((tb, 8, WIDE), F32)],
        compiler_params=_cparams(("arbitrary",)),
        name="peer_hid",
    )(rows, tab_u, x3d, gate_wide)


def _peer_out_kernel(row_smem, tab_ref, a_ref, x1_ref, o_ref, gbuf_a, gbuf_b, *, tb):
    row8 = lax.broadcasted_iota(I32, (8, WIDE), 0)

    def token(t, gbuf):
        a_even = jnp.broadcast_to(a_ref[pl.ds(t, 1), :], (8, WIDE))
        a_odd = pltpu.roll(a_even, 1, 1)
        acc = jnp.zeros((8, 128), F32)
        for s in range(PEER_ROW_WORDS):
            lhs = jnp.where(row8 == s, a_even, jnp.where(row8 == PEER_ROW_WORDS + s, a_odd, 0.0))
            acc = acc + jnp.dot(lhs.astype(BF16), _gathered_chunk(gbuf, s), preferred_element_type=F32)
        o_ref[t] = x1_ref[t] + acc

    _pipelined_tokens(tb, row_smem, tab_ref, gbuf_a, gbuf_b, token)


def _peer_out(rows, a_wide, tab_v, x1_3d, tb=128):
    T = rows.shape[0]
    return pl.pallas_call(
        functools.partial(_peer_out_kernel, tb=tb),
        grid=(T // tb,),
        in_specs=[pl.BlockSpec((tb, PEER_PAIRS), lambda i: (i, 0), memory_space=pltpu.SMEM),
                  _table_spec(tab_v),
                  pl.BlockSpec((tb, WIDE), lambda i: (i, 0)),
                  pl.BlockSpec((tb, 8, 128), lambda i: (i, 0, 0))],
        out_specs=pl.BlockSpec((tb, 8, 128), lambda i: (i, 0, 0)),
        out_shape=jax.ShapeDtypeStruct((T, 8, 128), F32),
        scratch_shapes=[pltpu.VMEM((GATHER_ROWS, 128), I32),
                        pltpu.VMEM((GATHER_ROWS, 128), I32)],
        compiler_params=_cparams(("arbitrary",)),
        name="peer_out",
    )(rows, tab_v, a_wide, x1_3d)


def _regroup_w_in(w):
    hg, att = 4 * HG_WIDTH, 3 * ATT_WIDTH
    return jnp.concatenate([w[:, hg + att:], w[:, :hg], w[:, hg:hg + att]], axis=1)


def kernel(x, norm1_g, w_in, hg_norm_g, hg_lb_logits, q_norm_g, k_norm_g, w_branch_a, w_branch_b,
           w_out, norm2_g, peer_wq, peer_subkeys, peer_u, peer_v):
    B, S, D = x.shape
    T = B * S
    assert D == D_MODEL and w_in.shape[0] == 1, "single-layer block of width 1024"
    x2d = x.reshape(T, D)

    proj = _in_proj(x2d, norm1_g[0], _regroup_w_in(w_in[0]).astype(BF16))
    y_a = _hgrn(proj, hg_lb_logits, hg_norm_g[0], B, S)
    y_b = _attn(proj, q_norm_g[0], k_norm_g[0], B, S)
    x1, h2, qp = _merge(x2d, proj, y_a, y_b, w_branch_a[0].astype(BF16), w_branch_b[0].astype(BF16),
                        w_out[0].astype(BF16), norm2_g[0], peer_wq[0].astype(BF16))
    idx_t, gate_t = _peer_topk(qp, peer_subkeys[0].astype(BF16))
    rows = idx_t.T * PEER_ROW_WORDS
    gate = gate_t.T
    gate_wide = jnp.stack([gate, jnp.zeros_like(gate)], axis=-1).reshape(T, WIDE)
    a_wide = _peer_hid(rows, _pack_table(peer_u[0]), h2.reshape(T, 8, 128), gate_wide)
    out = _peer_out(rows, a_wide, _pack_table(peer_v[0]), x1.reshape(T, 8, 128))
    return out.reshape(B, S, D)
```

```python
import functools

import jax
import jax.numpy as jnp
from jax import lax
from jax.experimental import pallas as pl
from jax.experimental.pallas import tpu as pltpu

F32 = jnp.float32
BF16 = jnp.bfloat16
I32 = jnp.int32

NORM_EPS = 1e-6
D_MODEL = 1024
HG_HEADS = 4
HG_DIM = 128
HG_CHUNK = 64
HG_WIDTH = HG_HEADS * HG_DIM
ATT_GROUPS = ((128, 1), (512, 4), (2048, 16))
ATT_BLOCK = 128
ATT_INTERLEAVE = 8
ATT_HEAD_DIM = 64
ATT_OUT_WIDTH = 256
ATT_WIDTH = 768
PEER_HEADS = 8
PEER_NKEYS = 128
PEER_HALF = 64
PEER_TOPK = 16
PEER_PAIRS = PEER_HEADS * PEER_TOPK
PEER_HALF_D = D_MODEL // 2

COL_GATE_A = 0
COL_GATE_B = D_MODEL
COL_HG = 2 * D_MODEL
COL_ATT = 2 * D_MODEL + 4 * HG_WIDTH
IN_COLS = COL_ATT + 3 * ATT_WIDTH

VMEM_LIMIT = 48 * 1024 * 1024


def _cparams(sem, vmem=VMEM_LIMIT):
    return pltpu.CompilerParams(dimension_semantics=sem, vmem_limit_bytes=vmem)


def _in_proj_kernel(x_ref, g_ref, w_ref, o_ref, h_scr):
    @pl.when(pl.program_id(1) == 0)
    def _():
        x = x_ref[...]
        ms = jnp.mean(x * x, axis=-1, keepdims=True)
        h_scr[...] = (x * lax.rsqrt(ms + NORM_EPS) * g_ref[...]).astype(BF16)

    o_ref[...] = jnp.dot(h_scr[...], w_ref[...], preferred_element_type=F32)


def _in_proj(x2d, g, w_bf16, tm=1024, tn=1280):
    T, D = x2d.shape
    N = w_bf16.shape[1]
    return pl.pallas_call(
        _in_proj_kernel,
        grid=(T // tm, N // tn),
        in_specs=[
            pl.BlockSpec((tm, D), lambda i, j: (i, 0)),
            pl.BlockSpec((1, D), lambda i, j: (0, 0)),
            pl.BlockSpec((D, tn), lambda i, j: (0, j)),
        ],
        out_specs=pl.BlockSpec((tm, tn), lambda i, j: (i, j)),
        out_shape=jax.ShapeDtypeStruct((T, N), F32),
        scratch_shapes=[pltpu.VMEM((tm, D), BF16)],
        compiler_params=_cparams(("parallel", "arbitrary")),
        name="in_proj",
    )(x2d, g.reshape(1, D), w_bf16)


def _split_bf16(x):
    hi = x.astype(BF16)
    lo = (x - hi.astype(F32)).astype(BF16)
    return hi, lo


def _hgrn_kernel(q_ref, f_ref, i_ref, og_ref, lbl_ref, ng_ref, o_ref, state_scr, *, n_chunks):
    @pl.when(pl.program_id(1) == 0)
    def _():
        state_scr[...] = jnp.zeros_like(state_scr)

    logits = lbl_ref[...]
    mx = jnp.max(logits, axis=0, keepdims=True)
    ex = jnp.exp(logits - mx)
    lb = ex[0:1, :] / jnp.sum(ex, axis=0, keepdims=True)
    ng = ng_ref[...]

    C = HG_CHUNK
    row = lax.broadcasted_iota(I32, (C, C), 0)
    col = lax.broadcasted_iota(I32, (C, C), 1)
    causal = row >= col
    tri = causal.astype(BF16)

    def chunk(c, carry):
        r0 = pl.multiple_of(c * C, C)
        rows = pl.ds(r0, C)
        f = lb + (1.0 - lb) * jax.nn.sigmoid(f_ref[rows, :])
        log_f = jnp.log(f)
        k = 1.0 - f
        q = jax.nn.sigmoid(q_ref[rows, :])
        v = i_ref[rows, :]
        og = og_ref[rows, :]
        lf_hi, lf_lo = _split_bf16(log_f)
        G = (jnp.dot(tri, lf_hi, preferred_element_type=F32)
             + jnp.dot(tri, lf_lo, preferred_element_type=F32))
        G_last = G[C - 1:C, :]
        q_dec = (q * jnp.exp(G)).astype(BF16)
        k_intra = (k * jnp.exp(-G)).astype(BF16)
        k_state = (k * jnp.exp(G_last - G)).astype(BF16)
        decay = jnp.exp(G_last)
        v_b = v.astype(BF16)
        for h in range(HG_HEADS):
            sl = slice(h * HG_DIM, (h + 1) * HG_DIM)
            st = state_scr[h]
            A = pl.dot(q_dec[:, sl], k_intra[:, sl], trans_b=True)
            A = jnp.where(causal, A, 0.0)
            o = jnp.dot(A.astype(BF16), v_b[:, sl], preferred_element_type=F32)
            o = o + pl.dot(q_dec[:, sl], st.astype(BF16), trans_b=True)
            state_scr[h] = decay[:, sl] * st + pl.dot(v_b[:, sl], k_state[:, sl], trans_a=True)
            o = o * lax.rsqrt(jnp.mean(o * o, axis=-1, keepdims=True) + NORM_EPS) * ng
            o_ref[rows, sl] = o * jax.nn.silu(og[:, sl])
        return carry

    lax.fori_loop(0, n_chunks, chunk, 0)


def _hgrn(proj, lb_logits, norm_g, B, S, ts=512):
    T = B * S
    nt = S // ts
    blk0 = COL_HG // HG_WIDTH

    def spec(kind):
        return pl.BlockSpec((ts, HG_WIDTH), lambda b, s, kind=kind: (b * nt + s, blk0 + kind))

    return pl.pallas_call(
        functools.partial(_hgrn_kernel, n_chunks=ts // HG_CHUNK),
        grid=(B, nt),
        in_specs=[spec(0), spec(1), spec(2), spec(3),
                  pl.BlockSpec(lb_logits.shape, lambda b, s: (0, 0)),
                  pl.BlockSpec((1, HG_DIM), lambda b, s: (0, 0))],
        out_specs=pl.BlockSpec((ts, HG_WIDTH), lambda b, s: (b * nt + s, 0)),
        out_shape=jax.ShapeDtypeStruct((T, HG_WIDTH), F32),
        scratch_shapes=[pltpu.VMEM((HG_HEADS, HG_DIM, HG_DIM), F32)],
        compiler_params=_cparams(("parallel", "arbitrary")),
        name="hgrn",
    )(proj, proj, proj, proj, lb_logits, norm_g.reshape(1, HG_DIM))


def _att_group(q_ref, k_ref, v_ref, gq, gk, acc_scr, m_scr, l_scr, qn_scr, kp_scr, vp_scr, *, S, d):
    L = S // d
    n_blk = L // ATT_BLOCK
    BLK = ATT_BLOCK
    lane = lax.broadcasted_iota(I32, (1, 2 * ATT_HEAD_DIM), 1)
    head_mask = [(lane < ATT_HEAD_DIM), (lane >= ATT_HEAD_DIM)]
    qi = lax.broadcasted_iota(I32, (BLK, 2 * BLK), 0)
    kj = lax.broadcasted_iota(I32, (BLK, 2 * BLK), 1)
    dist = qi - kj + BLK
    band = (dist >= 0) & (dist <= BLK)
    scale = ATT_HEAD_DIM ** -0.5

    def head_rms(t, g):
        t2 = t * t
        inv = jnp.zeros_like(t)
        for hm in head_mask:
            ms = jnp.sum(jnp.where(hm, t2, 0.0), axis=-1, keepdims=True) * (1.0 / ATT_HEAD_DIM)
            inv = jnp.where(hm, lax.rsqrt(ms + NORM_EPS), inv)
        return t * inv * g

    def prepare(r, slot):
        rows_all = pl.ds(r, L, stride=d) if d > 1 else pl.ds(0, L)
        k0 = slot * (L + BLK)
        qn_scr[pl.ds(slot * L, L), :] = head_rms(q_ref[rows_all, :], gq).astype(BF16)
        kp_scr[pl.ds(k0, BLK), :] = jnp.zeros((BLK, 128), BF16)
        vp_scr[pl.ds(k0, BLK), :] = jnp.zeros((BLK, 128), BF16)
        kp_scr[pl.ds(k0 + BLK, L), :] = head_rms(k_ref[rows_all, :], gk).astype(BF16)
        vp_scr[pl.ds(k0 + BLK, L), :] = v_ref[rows_all, :].astype(BF16)

    def block_stats(slot, n):
        b0 = n * BLK
        if not isinstance(n, int):
            b0 = pl.multiple_of(b0, BLK)
        valid = band & ((kj >= BLK) | (n > 0))
        qb = qn_scr[pl.ds(slot * L + b0, BLK), :]
        k2 = kp_scr[pl.ds(slot * (L + BLK) + b0, 2 * BLK), :]
        v2 = vp_scr[pl.ds(slot * (L + BLK) + b0, 2 * BLK), :]
        m_new = jnp.zeros((BLK, 128), F32)
        l_new = jnp.zeros((BLK, 128), F32)
        o_new = jnp.zeros((BLK, 128), F32)
        for hm in head_mask:
            qm = jnp.where(hm, qb, jnp.zeros_like(qb))
            s = pl.dot(qm, k2, trans_b=True) * scale
            s = jnp.where(valid, s, -jnp.inf)
            mh = jnp.max(s, axis=-1, keepdims=True)
            p = jnp.exp(s - mh)
            lh = jnp.sum(p, axis=-1, keepdims=True)
            oh = jnp.dot(p.astype(BF16), v2, preferred_element_type=F32)
            m_new = jnp.where(hm, mh, m_new)
            l_new = jnp.where(hm, lh, l_new)
            o_new = jnp.where(hm, oh, o_new)
        return m_new, l_new, o_new

    def fold(items):
        stats = [block_stats(slot, n) for _, slot, n in items]
        rows = [pl.ds(r + d * n * BLK, BLK, stride=d) if d > 1 else pl.ds(n * BLK, BLK)
                for r, _, n in items]
        old = [(m_scr[rw, :], l_scr[rw, :], acc_scr[rw, :]) for rw in rows]
        for rw, (m_new, l_new, o_new), (m_old, l_old, acc_old) in zip(rows, stats, old):
            m_tot = jnp.maximum(m_old, m_new)
            a_old = jnp.exp(m_old - m_tot)
            a_new = jnp.exp(m_new - m_tot)
            acc_scr[rw, :] = acc_old * a_old + o_new * a_new
            l_scr[rw, :] = l_old * a_old + l_new * a_new
            m_scr[rw, :] = m_tot

    U = ATT_INTERLEAVE
    if n_blk >= U:
        def residue(r, carry):
            prepare(r, 0)

            def blocks(i, carry2):
                fold([(r, 0, i * U + u) for u in range(U)])
                return carry2

            lax.fori_loop(0, n_blk // U, blocks, 0)
            return carry

        lax.fori_loop(0, d, residue, 0)
    else:
        per_step = U // n_blk

        def residues(i, carry):
            for u in range(per_step):
                prepare(i * per_step + u, u)
            fold([(i * per_step + u, u, n) for u in range(per_step) for n in range(n_blk)])
            return carry

        lax.fori_loop(0, d // per_step, residues, 0)


def _attn_kernel(q_ref, k_ref, v_ref, gq_ref, gk_ref, o_ref,
                 acc_scr, m_scr, l_scr, qn_scr, kp_scr, vp_scr, *, S):
    g = pl.program_id(2)

    @pl.when(g == 0)
    def _():
        acc_scr[...] = jnp.zeros_like(acc_scr)
        l_scr[...] = jnp.zeros_like(l_scr)
        m_scr[...] = jnp.full(m_scr.shape, -jnp.inf, F32)

    gq = gq_ref[...]
    gk = gk_ref[...]
    for gi, (w, d) in enumerate(ATT_GROUPS):
        assert w // d == ATT_BLOCK

        @pl.when(g == gi)
        def _(d=d):
            _att_group(q_ref, k_ref, v_ref, gq, gk, acc_scr, m_scr, l_scr,
                       qn_scr, kp_scr, vp_scr, S=S, d=d)

    @pl.when(g == len(ATT_GROUPS) - 1)
    def _():
        o_ref[...] = acc_scr[...] / l_scr[...]


def _attn(proj, q_norm_g, k_norm_g, B, S):
    T = B * S
    W = 2 * ATT_HEAD_DIM
    blk0 = COL_ATT // W
    per_kind = ATT_WIDTH // W
    per_group = ATT_OUT_WIDTH // W

    def spec(kind):
        return pl.BlockSpec((S, W), lambda b, j, g, kind=kind: (b, blk0 + kind * per_kind + g * per_group + j))

    gq2 = jnp.tile(q_norm_g.reshape(1, ATT_HEAD_DIM), (1, 2))
    gk2 = jnp.tile(k_norm_g.reshape(1, ATT_HEAD_DIM), (1, 2))
    return pl.pallas_call(
        functools.partial(_attn_kernel, S=S),
        grid=(B, per_group, len(ATT_GROUPS)),
        in_specs=[spec(0), spec(1), spec(2),
                  pl.BlockSpec((1, W), lambda b, j, g: (0, 0)),
                  pl.BlockSpec((1, W), lambda b, j, g: (0, 0))],
        out_specs=pl.BlockSpec((S, W), lambda b, j, g: (b, j)),
        out_shape=jax.ShapeDtypeStruct((T, ATT_OUT_WIDTH), F32),
        scratch_shapes=[pltpu.VMEM((S, W), F32), pltpu.VMEM((S, W), F32), pltpu.VMEM((S, W), F32),
                        pltpu.VMEM((S, W), BF16),
                        pltpu.VMEM((S + ATT_BLOCK, W), BF16),
                        pltpu.VMEM((S + ATT_BLOCK, W), BF16)],
        compiler_params=_cparams(("parallel", "parallel", "arbitrary")),
        name="attn",
    )(proj, proj, proj, gq2, gk2)


def _merge_kernel(x_ref, ga_ref, gb_ref, ya_ref, yb_ref, wa_ref, wb_ref, wo_ref, g2_ref, wq_ref,
                  x1_ref, h2_ref, qp_ref):
    pa = jnp.dot(ya_ref[...].astype(BF16), wa_ref[...], preferred_element_type=F32)
    pb = jnp.dot(yb_ref[...].astype(BF16), wb_ref[...], preferred_element_type=F32)
    merged = jax.nn.sigmoid(ga_ref[...]) * pa + jax.nn.sigmoid(gb_ref[...]) * pb
    x1 = x_ref[...] + jnp.dot(merged.astype(BF16), wo_ref[...], preferred_element_type=F32)
    x1_ref[...] = x1
    ms = jnp.mean(x1 * x1, axis=-1, keepdims=True)
    h2 = x1 * lax.rsqrt(ms + NORM_EPS) * g2_ref[...]
    h2_ref[...] = h2
    qp_ref[...] = jnp.dot(h2.astype(BF16), wq_ref[...], preferred_element_type=F32)


def _merge(x2d, proj, y_a, y_b, wa, wb, wo, g2, wq, tm=256):
    T, D = x2d.shape
    row = lambda i: (i, 0)
    const = lambda i: (0, 0)
    out_sd = jax.ShapeDtypeStruct((T, D), F32)
    return pl.pallas_call(
        _merge_kernel,
        grid=(T // tm,),
        in_specs=[
            pl.BlockSpec((tm, D), row),
            pl.BlockSpec((tm, D), lambda i: (i, COL_GATE_A // D_MODEL)),
            pl.BlockSpec((tm, D), lambda i: (i, COL_GATE_B // D_MODEL)),
            pl.BlockSpec((tm, HG_WIDTH), row),
            pl.BlockSpec((tm, ATT_OUT_WIDTH), row),
            pl.BlockSpec(wa.shape, const),
            pl.BlockSpec(wb.shape, const),
            pl.BlockSpec(wo.shape, const),
            pl.BlockSpec((1, D), const),
            pl.BlockSpec(wq.shape, const),
        ],
        out_specs=[pl.BlockSpec((tm, D), row)] * 3,
        out_shape=[out_sd, out_sd, out_sd],
        compiler_params=_cparams(("parallel",)),
        name="merge",
    )(x2d, proj, proj, y_a, y_b, wa, wb, wo, g2.reshape(1, D), wq)


def _top16_cols(s, code):
    big = jnp.float32(2 ** 30)
    vals, codes = [], []
    for _ in range(PEER_TOPK):
        m = jnp.max(s, axis=0, keepdims=True)
        c = jnp.min(jnp.where(s == m, code, big), axis=0, keepdims=True)
        s = jnp.where(code == c, -jnp.inf, s)
        vals.append(m)
        codes.append(c)
    return vals, codes


_CAND = [(a, b) for a in range(PEER_TOPK) for b in range(PEER_TOPK) if (a + 1) * (b + 1) <= PEER_TOPK]
_CAND_ROWS = -(-len(_CAND) // 8) * 8


def _peer_topk_kernel(qp_ref, keys_ref, code_ref, idx_ref, gate_ref):
    n = qp_ref.shape[0]
    key_code = lax.broadcasted_iota(I32, (PEER_NKEYS, 1), 0).astype(F32)
    cand_code = code_ref[...]
    pad = [jnp.full((1, n), -jnp.inf, F32)] * (_CAND_ROWS - len(_CAND))
    pad_i = [jnp.zeros((1, n), F32)] * (_CAND_ROWS - len(_CAND))
    for h in range(PEER_HEADS):
        sel = []
        for p in range(2):
            q = qp_ref[:, pl.ds((2 * h + p) * PEER_HALF, PEER_HALF)].astype(BF16)
            s = pl.dot(keys_ref[h, p], q, trans_b=True)
            sel.append(_top16_cols(s, key_code))
        (v1, i1), (v2, i2) = sel
        cand_s = jnp.concatenate([v1[a] + v2[b] for a, b in _CAND] + pad, axis=0)
        cand_i = jnp.concatenate([i1[a] * PEER_NKEYS + i2[b] for a, b in _CAND] + pad_i, axis=0)
        top_s, top_c = _top16_cols(cand_s, cand_code)
        ids = [jnp.sum(jnp.where(cand_code == c, cand_i, 0.0), axis=0, keepdims=True) for c in top_c]
        ts = jnp.concatenate(top_s, axis=0)
        e = jnp.exp(ts - ts[0:1, :])
        gate = e / jnp.sum(e, axis=0, keepdims=True)
        rows = pl.ds(h * PEER_TOPK, PEER_TOPK)
        idx_ref[rows, :] = jnp.concatenate(ids, axis=0).astype(I32)
        gate_ref[rows, :] = gate


def _peer_topk(qp, keys_bf16, tm=256):
    T, D = qp.shape
    codes = jnp.array([a * PEER_TOPK + b for a, b in _CAND]
                      + [2 ** 20 + i for i in range(_CAND_ROWS - len(_CAND))], F32).reshape(_CAND_ROWS, 1)
    return pl.pallas_call(
        _peer_topk_kernel,
        grid=(T // tm,),
        in_specs=[pl.BlockSpec((tm, D), lambda i: (i, 0)),
                  pl.BlockSpec(keys_bf16.shape, lambda i: (0, 0, 0, 0)),
                  pl.BlockSpec((_CAND_ROWS, 1), lambda i: (0, 0))],
        out_specs=[pl.BlockSpec((PEER_PAIRS, tm), lambda i: (0, i))] * 2,
        out_shape=[jax.ShapeDtypeStruct((PEER_PAIRS, T), I32),
                   jax.ShapeDtypeStruct((PEER_PAIRS, T), F32)],
        compiler_params=_cparams(("parallel",)),
        name="peer_topk",
    )(qp, keys_bf16, codes)


PEER_ROW_WORDS = 4
GATHER_ROWS = PEER_PAIRS * PEER_ROW_WORDS
WIDE = 2 * PEER_PAIRS


def _pack_table(w):
    n = w.shape[0]

    def bf16_bits(part):
        return lax.bitcast_convert_type(part.astype(BF16).astype(F32), jnp.uint32)

    packed = (bf16_bits(w[:, :PEER_HALF_D]) >> 16) | bf16_bits(w[:, PEER_HALF_D:])
    return lax.bitcast_convert_type(packed, I32).reshape(n * PEER_ROW_WORDS, 128)


def _table_spec(tab):
    return pl.BlockSpec(tab.shape, lambda i: (0, 0), pipeline_mode=pl.Buffered(1))


def _gather_rows(row_smem, t, tab_ref, gbuf):
    for j in range(PEER_PAIRS):
        r = pl.multiple_of(row_smem[t, j], PEER_ROW_WORDS)
        gbuf[pl.ds(PEER_ROW_WORDS * j, PEER_ROW_WORDS), :] = tab_ref[pl.ds(r, PEER_ROW_WORDS), :]


def _pipelined_tokens(tb, row_smem, tab_ref, gbuf_a, gbuf_b, compute):
    _gather_rows(row_smem, 0, tab_ref, gbuf_a)
    _gather_rows(row_smem, 1, tab_ref, gbuf_b)

    def two_tokens(i, carry):
        t = 2 * i
        compute(t, gbuf_a)
        _gather_rows(row_smem, jnp.minimum(t + 2, tb - 1), tab_ref, gbuf_a)
        compute(t + 1, gbuf_b)
        _gather_rows(row_smem, jnp.minimum(t + 3, tb - 1), tab_ref, gbuf_b)
        return carry

    lax.fori_loop(0, tb // 2, two_tokens, 0)


def _gathered_chunk(gbuf, s):
    return pltpu.bitcast(gbuf[pl.ds(s, PEER_PAIRS, stride=PEER_ROW_WORDS), :], BF16)


def _gelu_exact(x):
    return 0.5 * x * (1.0 + lax.erf(x * (2.0 ** -0.5)))


def _peer_hid_kernel(row_smem, tab_ref, x_ref, gate_ref, a_ref, gbuf_a, gbuf_b, raw_scr, *, tb):
    def token(t, gbuf):
        acc = jnp.zeros((8, WIDE), F32)
        for s in range(PEER_ROW_WORDS):
            xs = x_ref[t, pl.ds(s, 2, stride=PEER_ROW_WORDS), :]
            xs_hi = xs.astype(BF16).astype(F32)
            lhs = jnp.concatenate([xs_hi, xs - xs_hi, jnp.zeros((4, 128), F32)], axis=0).astype(BF16)
            acc = acc + pl.dot(lhs, _gathered_chunk(gbuf, s), trans_b=True)
        raw_scr[t] = acc

    _pipelined_tokens(tb, row_smem, tab_ref, gbuf_a, gbuf_b, token)
    lo = raw_scr[:, 0, :] + raw_scr[:, 2, :]
    hi = raw_scr[:, 1, :] + raw_scr[:, 3, :]
    hid = lo + pltpu.roll(hi, WIDE - 1, 1)
    a_ref[...] = _gelu_exact(hid) * gate_ref[...]


def _peer_hid(rows, tab_u, x3d, gate_wide, tb=256):
    T = rows.shape[0]
    return pl.pallas_call(
        functools.partial(_peer_hid_kernel, tb=tb),
        grid=(T // tb,),
        in_specs=[pl.BlockSpec((tb, PEER_PAIRS), lambda i: (i, 0), memory_space=pltpu.SMEM),
                  _table_spec(tab_u),
                  pl.BlockSpec((tb, 8, 128), lambda i: (i, 0, 0)),
                  pl.BlockSpec((tb, WIDE), lambda i: (i, 0))],
        out_specs=pl.BlockSpec((tb, WIDE), lambda i: (i, 0)),
        out_shape=jax.ShapeDtypeStruct((T, WIDE), F32),
        scratch_shapes=[pltpu.VMEM((GATHER_ROWS, 128), I32),
                        pltpu.VMEM((GATHER_ROWS, 128), I32),
                        pltpu.VMEM((tb, 8, WIDE), F32)],
        compiler_params=_cparams(("arbitrary",)),
        name="peer_hid",
    )(rows, tab_u, x3d, gate_wide)


def _peer_out_kernel(row_smem, tab_ref, a_ref, x1_ref, o_ref, gbuf_a, gbuf_b, *, tb):
    row8 = lax.broadcasted_iota(I32, (8, WIDE), 0)

    def token(t, gbuf):
        a_even = jnp.broadcast_to(a_ref[pl.ds(t, 1), :], (8, WIDE))
        a_odd = pltpu.roll(a_even, 1, 1)
        acc = jnp.zeros((8, 128), F32)
        for s in range(PEER_ROW_WORDS):
            lhs = jnp.where(row8 == s, a_even, jnp.where(row8 == PEER_ROW_WORDS + s, a_odd, 0.0))
            acc = acc + jnp.dot(lhs.astype(BF16), _gathered_chunk(gbuf, s), preferred_element_type=F32)
        o_ref[t] = x1_ref[t] + acc

    _pipelined_tokens(tb, row_smem, tab_ref, gbuf_a, gbuf_b, token)


def _peer_out(rows, a_wide, tab_v, x1_3d, tb=256):
    T = rows.shape[0]
    return pl.pallas_call(
        functools.partial(_peer_out_kernel, tb=tb),
        grid=(T // tb,),
        in_specs=[pl.BlockSpec((tb, PEER_PAIRS), lambda i: (i, 0), memory_space=pltpu.SMEM),
                  _table_spec(tab_v),
                  pl.BlockSpec((tb, WIDE), lambda i: (i, 0)),
                  pl.BlockSpec((tb, 8, 128), lambda i: (i, 0, 0))],
        out_specs=pl.BlockSpec((tb, 8, 128), lambda i: (i, 0, 0)),
        out_shape=jax.ShapeDtypeStruct((T, 8, 128), F32),
        scratch_shapes=[pltpu.VMEM((GATHER_ROWS, 128), I32),
                        pltpu.VMEM((GATHER_ROWS, 128), I32)],
        compiler_params=_cparams(("arbitrary",)),
        name="peer_out",
    )(rows, tab_v, a_wide, x1_3d)


def _regroup_w_in(w):
    hg, att = 4 * HG_WIDTH, 3 * ATT_WIDTH
    return jnp.concatenate([w[:, hg + att:], w[:, :hg], w[:, hg:hg + att]], axis=1)


def kernel(x, norm1_g, w_in, hg_norm_g, hg_lb_logits, q_norm_g, k_norm_g, w_branch_a, w_branch_b,
           w_out, norm2_g, peer_wq, peer_subkeys, peer_u, peer_v):
    B, S, D = x.shape
    T = B * S
    assert D == D_MODEL and w_in.shape[0] == 1, "single-layer block of width 1024"
    x2d = x.reshape(T, D)

    proj = _in_proj(x2d, norm1_g[0], _regroup_w_in(w_in[0]).astype(BF16))
    y_a = _hgrn(proj, hg_lb_logits, hg_norm_g[0], B, S)
    y_b = _attn(proj, q_norm_g[0], k_norm_g[0], B, S)
    x1, h2, qp = _merge(x2d, proj, y_a, y_b, w_branch_a[0].astype(BF16), w_branch_b[0].astype(BF16),
                        w_out[0].astype(BF16), norm2_g[0], peer_wq[0].astype(BF16))
    idx_t, gate_t = _peer_topk(qp, peer_subkeys[0].astype(BF16))
    rows = idx_t.T * PEER_ROW_WORDS
    gate = gate_t.T
    gate_wide = jnp.stack([gate, jnp.zeros_like(gate)], axis=-1).reshape(T, WIDE)
    a_wide = _peer_hid(rows, _pack_table(peer_u[0]), h2.reshape(T, 8, 128), gate_wide)
    out = _peer_out(rows, a_wide, _pack_table(peer_v[0]), x1.reshape(T, 8, 128))
    return out.reshape(B, S, D)
```

```python
import functools

import jax
import jax.numpy as jnp
from jax import lax
from jax.experimental import pallas as pl
from jax.experimental.pallas import tpu as pltpu

F32 = jnp.float32
BF16 = jnp.bfloat16
I32 = jnp.int32

NORM_EPS = 1e-6
D_MODEL = 1024
HG_HEADS = 4
HG_DIM = 128
HG_CHUNK = 64
HG_WIDTH = HG_HEADS * HG_DIM
ATT_GROUPS = ((128, 1), (512, 4), (2048, 16))
ATT_BLOCK = 128
ATT_INTERLEAVE = 8
ATT_HEAD_DIM = 64
ATT_OUT_WIDTH = 256
ATT_WIDTH = 768
PEER_HEADS = 8
PEER_NKEYS = 128
PEER_HALF = 64
PEER_TOPK = 16
PEER_PAIRS = PEER_HEADS * PEER_TOPK
PEER_HALF_D = D_MODEL // 2

COL_HG = 0
COL_ATT = 4 * HG_WIDTH
IN_COLS = COL_ATT + 3 * ATT_WIDTH

VMEM_LIMIT = 48 * 1024 * 1024


def _cparams(sem, vmem=VMEM_LIMIT):
    return pltpu.CompilerParams(dimension_semantics=sem, vmem_limit_bytes=vmem)


def _in_proj_kernel(x_ref, g_ref, w_ref, o_ref, h_scr):
    @pl.when(pl.program_id(1) == 0)
    def _():
        x = x_ref[...]
        ms = jnp.mean(x * x, axis=-1, keepdims=True)
        h_scr[...] = (x * lax.rsqrt(ms + NORM_EPS) * g_ref[...]).astype(BF16)

    o_ref[...] = jnp.dot(h_scr[...], w_ref[...], preferred_element_type=F32)


def _in_proj(x2d, g, w_bf16, tm=1024, tn=2176):
    T, D = x2d.shape
    N = w_bf16.shape[1]
    return pl.pallas_call(
        _in_proj_kernel,
        grid=(T // tm, N // tn),
        in_specs=[
            pl.BlockSpec((tm, D), lambda i, j: (i, 0)),
            pl.BlockSpec((1, D), lambda i, j: (0, 0)),
            pl.BlockSpec((D, tn), lambda i, j: (0, j)),
        ],
        out_specs=pl.BlockSpec((tm, tn), lambda i, j: (i, j)),
        out_shape=jax.ShapeDtypeStruct((T, N), F32),
        scratch_shapes=[pltpu.VMEM((tm, D), BF16)],
        compiler_params=_cparams(("parallel", "arbitrary")),
        name="in_proj",
    )(x2d, g.reshape(1, D), w_bf16)


def _split_bf16(x):
    hi = x.astype(BF16)
    lo = (x - hi.astype(F32)).astype(BF16)
    return hi, lo


def _hgrn_kernel(q_ref, f_ref, i_ref, og_ref, lbl_ref, ng_ref, o_ref, state_scr, *, n_chunks):
    @pl.when(pl.program_id(1) == 0)
    def _():
        state_scr[...] = jnp.zeros_like(state_scr)

    logits = lbl_ref[...]
    mx = jnp.max(logits, axis=0, keepdims=True)
    ex = jnp.exp(logits - mx)
    lb = ex[0:1, :] / jnp.sum(ex, axis=0, keepdims=True)
    ng = ng_ref[...]

    C = HG_CHUNK
    row = lax.broadcasted_iota(I32, (C, C), 0)
    col = lax.broadcasted_iota(I32, (C, C), 1)
    causal = row >= col
    tri = causal.astype(BF16)

    def chunk(c, carry):
        r0 = pl.multiple_of(c * C, C)
        rows = pl.ds(r0, C)
        f = lb + (1.0 - lb) * jax.nn.sigmoid(f_ref[rows, :])
        log_f = jnp.log(f)
        k = 1.0 - f
        q = jax.nn.sigmoid(q_ref[rows, :])
        v = i_ref[rows, :]
        og = og_ref[rows, :]
        lf_hi, lf_lo = _split_bf16(log_f)
        G = (jnp.dot(tri, lf_hi, preferred_element_type=F32)
             + jnp.dot(tri, lf_lo, preferred_element_type=F32))
        G_last = G[C - 1:C, :]
        q_dec = (q * jnp.exp(G)).astype(BF16)
        k_intra = (k * jnp.exp(-G)).astype(BF16)
        k_state = (k * jnp.exp(G_last - G)).astype(BF16)
        decay = jnp.exp(G_last)
        v_b = v.astype(BF16)
        for h in range(HG_HEADS):
            sl = slice(h * HG_DIM, (h + 1) * HG_DIM)
            st = state_scr[h]
            A = pl.dot(q_dec[:, sl], k_intra[:, sl], trans_b=True)
            A = jnp.where(causal, A, 0.0)
            o = jnp.dot(A.astype(BF16), v_b[:, sl], preferred_element_type=F32)
            o = o + pl.dot(q_dec[:, sl], st.astype(BF16), trans_b=True)
            state_scr[h] = decay[:, sl] * st + pl.dot(v_b[:, sl], k_state[:, sl], trans_a=True)
            o = o * lax.rsqrt(jnp.mean(o * o, axis=-1, keepdims=True) + NORM_EPS) * ng
            o_ref[rows, sl] = o * jax.nn.silu(og[:, sl])
        return carry

    lax.fori_loop(0, n_chunks, chunk, 0)


def _hgrn(proj, lb_logits, norm_g, B, S, ts=512):
    T = B * S
    nt = S // ts
    blk0 = COL_HG // HG_WIDTH

    def spec(kind):
        return pl.BlockSpec((ts, HG_WIDTH), lambda b, s, kind=kind: (b * nt + s, blk0 + kind))

    return pl.pallas_call(
        functools.partial(_hgrn_kernel, n_chunks=ts // HG_CHUNK),
        grid=(B, nt),
        in_specs=[spec(0), spec(1), spec(2), spec(3),
                  pl.BlockSpec(lb_logits.shape, lambda b, s: (0, 0)),
                  pl.BlockSpec((1, HG_DIM), lambda b, s: (0, 0))],
        out_specs=pl.BlockSpec((ts, HG_WIDTH), lambda b, s: (b * nt + s, 0)),
        out_shape=jax.ShapeDtypeStruct((T, HG_WIDTH), F32),
        scratch_shapes=[pltpu.VMEM((HG_HEADS, HG_DIM, HG_DIM), F32)],
        compiler_params=_cparams(("parallel", "arbitrary")),
        name="hgrn",
    )(proj, proj, proj, proj, lb_logits, norm_g.reshape(1, HG_DIM))


def _att_group(q_ref, k_ref, v_ref, gq, gk, acc_scr, m_scr, l_scr, qn_scr, kp_scr, vp_scr, *, S, d):
    L = S // d
    n_blk = L // ATT_BLOCK
    BLK = ATT_BLOCK
    lane = lax.broadcasted_iota(I32, (1, 2 * ATT_HEAD_DIM), 1)
    head_mask = [(lane < ATT_HEAD_DIM), (lane >= ATT_HEAD_DIM)]
    qi = lax.broadcasted_iota(I32, (BLK, 2 * BLK), 0)
    kj = lax.broadcasted_iota(I32, (BLK, 2 * BLK), 1)
    dist = qi - kj + BLK
    band = (dist >= 0) & (dist <= BLK)
    scale = ATT_HEAD_DIM ** -0.5

    def head_rms(t, g):
        t2 = t * t
        inv = jnp.zeros_like(t)
        for hm in head_mask:
            ms = jnp.sum(jnp.where(hm, t2, 0.0), axis=-1, keepdims=True) * (1.0 / ATT_HEAD_DIM)
            inv = jnp.where(hm, lax.rsqrt(ms + NORM_EPS), inv)
        return t * inv * g

    def prepare(r, slot):
        rows_all = pl.ds(r, L, stride=d) if d > 1 else pl.ds(0, L)
        k0 = slot * (L + BLK)
        qn_scr[pl.ds(slot * L, L), :] = head_rms(q_ref[rows_all, :], gq).astype(BF16)
        kp_scr[pl.ds(k0, BLK), :] = jnp.zeros((BLK, 128), BF16)
        vp_scr[pl.ds(k0, BLK), :] = jnp.zeros((BLK, 128), BF16)
        kp_scr[pl.ds(k0 + BLK, L), :] = head_rms(k_ref[rows_all, :], gk).astype(BF16)
        vp_scr[pl.ds(k0 + BLK, L), :] = v_ref[rows_all, :].astype(BF16)

    def block_stats(slot, n):
        b0 = n * BLK
        if not isinstance(n, int):
            b0 = pl.multiple_of(b0, BLK)
        valid = band & ((kj >= BLK) | (n > 0))
        qb = qn_scr[pl.ds(slot * L + b0, BLK), :]
        k2 = kp_scr[pl.ds(slot * (L + BLK) + b0, 2 * BLK), :]
        v2 = vp_scr[pl.ds(slot * (L + BLK) + b0, 2 * BLK), :]
        m_new = jnp.zeros((BLK, 128), F32)
        l_new = jnp.zeros((BLK, 128), F32)
        o_new = jnp.zeros((BLK, 128), F32)
        for hm in head_mask:
            qm = jnp.where(hm, qb, jnp.zeros_like(qb))
            s = pl.dot(qm, k2, trans_b=True) * scale
            s = jnp.where(valid, s, -jnp.inf)
            mh = jnp.max(s, axis=-1, keepdims=True)
            p = jnp.exp(s - mh)
            lh = jnp.sum(p, axis=-1, keepdims=True)
            oh = jnp.dot(p.astype(BF16), v2, preferred_element_type=F32)
            m_new = jnp.where(hm, mh, m_new)
            l_new = jnp.where(hm, lh, l_new)
            o_new = jnp.where(hm, oh, o_new)
        return m_new, l_new, o_new

    def fold(items):
        stats = [block_stats(slot, n) for _, slot, n in items]
        rows = [pl.ds(r + d * n * BLK, BLK, stride=d) if d > 1 else pl.ds(n * BLK, BLK)
                for r, _, n in items]
        old = [(m_scr[rw, :], l_scr[rw, :], acc_scr[rw, :]) for rw in rows]
        for rw, (m_new, l_new, o_new), (m_old, l_old, acc_old) in zip(rows, stats, old):
            m_tot = jnp.maximum(m_old, m_new)
            a_old = jnp.exp(m_old - m_tot)
            a_new = jnp.exp(m_new - m_tot)
            acc_scr[rw, :] = acc_old * a_old + o_new * a_new
            l_scr[rw, :] = l_old * a_old + l_new * a_new
            m_scr[rw, :] = m_tot

    U = ATT_INTERLEAVE
    if n_blk >= U:
        def residue(r, carry):
            prepare(r, 0)

            def blocks(i, carry2):
                fold([(r, 0, i * U + u) for u in range(U)])
                return carry2

            lax.fori_loop(0, n_blk // U, blocks, 0)
            return carry

        lax.fori_loop(0, d, residue, 0)
    else:
        per_step = U // n_blk

        def residues(i, carry):
            for u in range(per_step):
                prepare(i * per_step + u, u)
            fold([(i * per_step + u, u, n) for u in range(per_step) for n in range(n_blk)])
            return carry

        lax.fori_loop(0, d // per_step, residues, 0)


def _attn_kernel(q_ref, k_ref, v_ref, gq_ref, gk_ref, o_ref,
                 acc_scr, m_scr, l_scr, qn_scr, kp_scr, vp_scr, *, S):
    g = pl.program_id(2)

    @pl.when(g == 0)
    def _():
        acc_scr[...] = jnp.zeros_like(acc_scr)
        l_scr[...] = jnp.zeros_like(l_scr)
        m_scr[...] = jnp.full(m_scr.shape, -jnp.inf, F32)

    gq = gq_ref[...]
    gk = gk_ref[...]
    for gi, (w, d) in enumerate(ATT_GROUPS):
        assert w // d == ATT_BLOCK

        @pl.when(g == gi)
        def _(d=d):
            _att_group(q_ref, k_ref, v_ref, gq, gk, acc_scr, m_scr, l_scr,
                       qn_scr, kp_scr, vp_scr, S=S, d=d)

    @pl.when(g == len(ATT_GROUPS) - 1)
    def _():
        o_ref[...] = acc_scr[...] / l_scr[...]


def _attn(proj, q_norm_g, k_norm_g, B, S):
    T = B * S
    W = 2 * ATT_HEAD_DIM
    blk0 = COL_ATT // W
    per_kind = ATT_WIDTH // W
    per_group = ATT_OUT_WIDTH // W

    def spec(kind):
        return pl.BlockSpec((S, W), lambda b, j, g, kind=kind: (b, blk0 + kind * per_kind + g * per_group + j))

    gq2 = jnp.tile(q_norm_g.reshape(1, ATT_HEAD_DIM), (1, 2))
    gk2 = jnp.tile(k_norm_g.reshape(1, ATT_HEAD_DIM), (1, 2))
    return pl.pallas_call(
        functools.partial(_attn_kernel, S=S),
        grid=(B, per_group, len(ATT_GROUPS)),
        in_specs=[spec(0), spec(1), spec(2),
                  pl.BlockSpec((1, W), lambda b, j, g: (0, 0)),
                  pl.BlockSpec((1, W), lambda b, j, g: (0, 0))],
        out_specs=pl.BlockSpec((S, W), lambda b, j, g: (b, j)),
        out_shape=jax.ShapeDtypeStruct((T, ATT_OUT_WIDTH), F32),
        scratch_shapes=[pltpu.VMEM((S, W), F32), pltpu.VMEM((S, W), F32), pltpu.VMEM((S, W), F32),
                        pltpu.VMEM((S, W), BF16),
                        pltpu.VMEM((S + ATT_BLOCK, W), BF16),
                        pltpu.VMEM((S + ATT_BLOCK, W), BF16)],
        compiler_params=_cparams(("parallel", "parallel", "arbitrary")),
        name="attn",
    )(proj, proj, proj, gq2, gk2)


def _merge_kernel(x_ref, g1_ref, wg_ref, ya_ref, yb_ref, wa_ref, wb_ref, wo_ref, g2_ref, wq_ref,
                  x1_ref, h2_ref, qp_ref):
    x = x_ref[...]
    h = (x * lax.rsqrt(jnp.mean(x * x, axis=-1, keepdims=True) + NORM_EPS) * g1_ref[...]).astype(BF16)
    gates = jnp.dot(h, wg_ref[...], preferred_element_type=F32)
    pa = jnp.dot(ya_ref[...].astype(BF16), wa_ref[...], preferred_element_type=F32)
    pb = jnp.dot(yb_ref[...].astype(BF16), wb_ref[...], preferred_element_type=F32)
    merged = jax.nn.sigmoid(gates[:, :D_MODEL]) * pa + jax.nn.sigmoid(gates[:, D_MODEL:]) * pb
    x1 = x + jnp.dot(merged.astype(BF16), wo_ref[...], preferred_element_type=F32)
    x1_ref[...] = x1
    ms = jnp.mean(x1 * x1, axis=-1, keepdims=True)
    h2 = x1 * lax.rsqrt(ms + NORM_EPS) * g2_ref[...]
    h2_ref[...] = h2
    qp_ref[...] = jnp.dot(h2.astype(BF16), wq_ref[...], preferred_element_type=F32)


def _merge(x2d, g1, wg, y_a, y_b, wa, wb, wo, g2, wq, tm=256):
    T, D = x2d.shape
    row = lambda i: (i, 0)
    const = lambda i: (0, 0)
    out_sd = jax.ShapeDtypeStruct((T, D), F32)
    return pl.pallas_call(
        _merge_kernel,
        grid=(T // tm,),
        in_specs=[
            pl.BlockSpec((tm, D), row),
            pl.BlockSpec((1, D), const),
            pl.BlockSpec(wg.shape, const),
            pl.BlockSpec((tm, HG_WIDTH), row),
            pl.BlockSpec((tm, ATT_OUT_WIDTH), row),
            pl.BlockSpec(wa.shape, const),
            pl.BlockSpec(wb.shape, const),
            pl.BlockSpec(wo.shape, const),
            pl.BlockSpec((1, D), const),
            pl.BlockSpec(wq.shape, const),
        ],
        out_specs=[pl.BlockSpec((tm, D), row)] * 3,
        out_shape=[out_sd, out_sd, out_sd],
        compiler_params=_cparams(("parallel",)),
        name="merge",
    )(x2d, g1.reshape(1, D), wg, y_a, y_b, wa, wb, wo, g2.reshape(1, D), wq)


def _top16_cols(s, code):
    big = jnp.float32(2 ** 30)
    vals, codes = [], []
    for _ in range(PEER_TOPK):
        m = jnp.max(s, axis=0, keepdims=True)
        c = jnp.min(jnp.where(s == m, code, big), axis=0, keepdims=True)
        s = jnp.where(code == c, -jnp.inf, s)
        vals.append(m)
        codes.append(c)
    return vals, codes


_CAND = [(a, b) for a in range(PEER_TOPK) for b in range(PEER_TOPK) if (a + 1) * (b + 1) <= PEER_TOPK]
_CAND_ROWS = -(-len(_CAND) // 8) * 8


def _peer_topk_kernel(qp_ref, keys_ref, code_ref, idx_ref, gate_ref):
    n = qp_ref.shape[0]
    key_code = lax.broadcasted_iota(I32, (PEER_NKEYS, 1), 0).astype(F32)
    cand_code = code_ref[...]
    pad = [jnp.full((1, n), -jnp.inf, F32)] * (_CAND_ROWS - len(_CAND))
    pad_i = [jnp.zeros((1, n), F32)] * (_CAND_ROWS - len(_CAND))
    for h in range(PEER_HEADS):
        sel = []
        for p in range(2):
            q = qp_ref[:, pl.ds((2 * h + p) * PEER_HALF, PEER_HALF)].astype(BF16)
            s = pl.dot(keys_ref[h, p], q, trans_b=True)
            sel.append(_top16_cols(s, key_code))
        (v1, i1), (v2, i2) = sel
        cand_s = jnp.concatenate([v1[a] + v2[b] for a, b in _CAND] + pad, axis=0)
        cand_i = jnp.concatenate([i1[a] * PEER_NKEYS + i2[b] for a, b in _CAND] + pad_i, axis=0)
        top_s, top_c = _top16_cols(cand_s, cand_code)
        ids = [jnp.sum(jnp.where(cand_code == c, cand_i, 0.0), axis=0, keepdims=True) for c in top_c]
        ts = jnp.concatenate(top_s, axis=0)
        e = jnp.exp(ts - ts[0:1, :])
        gate = e / jnp.sum(e, axis=0, keepdims=True)
        rows = pl.ds(h * PEER_TOPK, PEER_TOPK)
        idx_ref[rows, :] = jnp.concatenate(ids, axis=0).astype(I32)
        gate_ref[rows, :] = gate


def _peer_topk(qp, keys_bf16, tm=256):
    T, D = qp.shape
    codes = jnp.array([a * PEER_TOPK + b for a, b in _CAND]
                      + [2 ** 20 + i for i in range(_CAND_ROWS - len(_CAND))], F32).reshape(_CAND_ROWS, 1)
    return pl.pallas_call(
        _peer_topk_kernel,
        grid=(T // tm,),
        in_specs=[pl.BlockSpec((tm, D), lambda i: (i, 0)),
                  pl.BlockSpec(keys_bf16.shape, lambda i: (0, 0, 0, 0)),
                  pl.BlockSpec((_CAND_ROWS, 1), lambda i: (0, 0))],
        out_specs=[pl.BlockSpec((PEER_PAIRS, tm), lambda i: (0, i))] * 2,
        out_shape=[jax.ShapeDtypeStruct((PEER_PAIRS, T), I32),
                   jax.ShapeDtypeStruct((PEER_PAIRS, T), F32)],
        compiler_params=_cparams(("parallel",)),
        name="peer_topk",
    )(qp, keys_bf16, codes)


PEER_ROW_WORDS = 4
GATHER_ROWS = PEER_PAIRS * PEER_ROW_WORDS
WIDE = 2 * PEER_PAIRS


def _pack_table(w):
    n = w.shape[0]

    def bf16_bits(part):
        return lax.bitcast_convert_type(part.astype(BF16).astype(F32), jnp.uint32)

    packed = (bf16_bits(w[:, :PEER_HALF_D]) >> 16) | bf16_bits(w[:, PEER_HALF_D:])
    return lax.bitcast_convert_type(packed, I32).reshape(n * PEER_ROW_WORDS, 128)


def _table_spec(tab):
    return pl.BlockSpec(tab.shape, lambda i: (0, 0), pipeline_mode=pl.Buffered(1))


def _gather_rows(row_smem, t, tab_ref, gbuf):
    for j in range(PEER_PAIRS):
        r = pl.multiple_of(row_smem[t, j], PEER_ROW_WORDS)
        gbuf[pl.ds(PEER_ROW_WORDS * j, PEER_ROW_WORDS), :] = tab_ref[pl.ds(r, PEER_ROW_WORDS), :]


def _pipelined_tokens(tb, row_smem, tab_ref, gbuf_a, gbuf_b, compute):
    _gather_rows(row_smem, 0, tab_ref, gbuf_a)
    _gather_rows(row_smem, 1, tab_ref, gbuf_b)

    def two_tokens(i, carry):
        t = 2 * i
        compute(t, gbuf_a)
        _gather_rows(row_smem, jnp.minimum(t + 2, tb - 1), tab_ref, gbuf_a)
        compute(t + 1, gbuf_b)
        _gather_rows(row_smem, jnp.minimum(t + 3, tb - 1), tab_ref, gbuf_b)
        return carry

    lax.fori_loop(0, tb // 2, two_tokens, 0)


def _gathered_chunk(gbuf, s):
    return pltpu.bitcast(gbuf[pl.ds(s, PEER_PAIRS, stride=PEER_ROW_WORDS), :], BF16)


def _gelu_exact(x):
    return 0.5 * x * (1.0 + lax.erf(x * (2.0 ** -0.5)))


def _peer_hid_kernel(row_smem, tab_ref, x_ref, gate_ref, a_ref, gbuf_a, gbuf_b, raw_scr, *, tb):
    def token(t, gbuf):
        acc = jnp.zeros((8, WIDE), F32)
        for s in range(PEER_ROW_WORDS):
            xs = x_ref[t, pl.ds(s, 2, stride=PEER_ROW_WORDS), :]
            xs_hi = xs.astype(BF16).astype(F32)
            lhs = jnp.concatenate([xs_hi, xs - xs_hi, jnp.zeros((4, 128), F32)], axis=0).astype(BF16)
            acc = acc + pl.dot(lhs, _gathered_chunk(gbuf, s), trans_b=True)
        raw_scr[t] = acc

    _pipelined_tokens(tb, row_smem, tab_ref, gbuf_a, gbuf_b, token)
    lo = raw_scr[:, 0, :] + raw_scr[:, 2, :]
    hi = raw_scr[:, 1, :] + raw_scr[:, 3, :]
    hid = lo + pltpu.roll(hi, WIDE - 1, 1)
    a_ref[...] = _gelu_exact(hid) * gate_ref[...]


def _peer_hid(rows, tab_u, x3d, gate_wide, tb=256):
    T = rows.shape[0]
    return pl.pallas_call(
        functools.partial(_peer_hid_kernel, tb=tb),
        grid=(T // tb,),
        in_specs=[pl.BlockSpec((tb, PEER_PAIRS), lambda i: (i, 0), memory_space=pltpu.SMEM),
                  _table_spec(tab_u),
                  pl.BlockSpec((tb, 8, 128), lambda i: (i, 0, 0)),
                  pl.BlockSpec((tb, WIDE), lambda i: (i, 0))],
        out_specs=pl.BlockSpec((tb, WIDE), lambda i: (i, 0)),
        out_shape=jax.ShapeDtypeStruct((T, WIDE), F32),
        scratch_shapes=[pltpu.VMEM((GATHER_ROWS, 128), I32),
                        pltpu.VMEM((GATHER_ROWS, 128), I32),
                        pltpu.VMEM((tb, 8, WIDE), F32)],
        compiler_params=_cparams(("arbitrary",)),
        name="peer_hid",
    )(rows, tab_u, x3d, gate_wide)


def _peer_out_kernel(row_smem, tab_ref, a_ref, x1_ref, o_ref, gbuf_a, gbuf_b, *, tb):
    row8 = lax.broadcasted_iota(I32, (8, WIDE), 0)

    def token(t, gbuf):
        a_even = jnp.broadcast_to(a_ref[pl.ds(t, 1), :], (8, WIDE))
        a_odd = pltpu.roll(a_even, 1, 1)
        acc = jnp.zeros((8, 128), F32)
        for s in range(PEER_ROW_WORDS):
            lhs = jnp.where(row8 == s, a_even, jnp.where(row8 == PEER_ROW_WORDS + s, a_odd, 0.0))
            acc = acc + jnp.dot(lhs.astype(BF16), _gathered_chunk(gbuf, s), preferred_element_type=F32)
        o_ref[t] = x1_ref[t] + acc

    _pipelined_tokens(tb, row_smem, tab_ref, gbuf_a, gbuf_b, token)


def _peer_out(rows, a_wide, tab_v, x1_3d, tb=256):
    T = rows.shape[0]
    return pl.pallas_call(
        functools.partial(_peer_out_kernel, tb=tb),
        grid=(T // tb,),
        in_specs=[pl.BlockSpec((tb, PEER_PAIRS), lambda i: (i, 0), memory_space=pltpu.SMEM),
                  _table_spec(tab_v),
                  pl.BlockSpec((tb, WIDE), lambda i: (i, 0)),
                  pl.BlockSpec((tb, 8, 128), lambda i: (i, 0, 0))],
        out_specs=pl.BlockSpec((tb, 8, 128), lambda i: (i, 0, 0)),
        out_shape=jax.ShapeDtypeStruct((T, 8, 128), F32),
        scratch_shapes=[pltpu.VMEM((GATHER_ROWS, 128), I32),
                        pltpu.VMEM((GATHER_ROWS, 128), I32)],
        compiler_params=_cparams(("arbitrary",)),
        name="peer_out",
    )(rows, tab_v, a_wide, x1_3d)


def _regroup_w_in(w):
    return w[:, :IN_COLS], w[:, IN_COLS:]


def kernel(x, norm1_g, w_in, hg_norm_g, hg_lb_logits, q_norm_g, k_norm_g, w_branch_a, w_branch_b,
           w_out, norm2_g, peer_wq, peer_subkeys, peer_u, peer_v):
    B, S, D = x.shape
    T = B * S
    assert D == D_MODEL and w_in.shape[0] == 1, "single-layer block of width 1024"
    x2d = x.reshape(T, D)

    w_mix, w_gate = _regroup_w_in(w_in[0])
    proj = _in_proj(x2d, norm1_g[0], w_mix.astype(BF16))
    y_a = _hgrn(proj, hg_lb_logits, hg_norm_g[0], B, S)
    y_b = _attn(proj, q_norm_g[0], k_norm_g[0], B, S)
    x1, h2, qp = _merge(x2d, norm1_g[0], w_gate.astype(BF16), y_a, y_b, w_branch_a[0].astype(BF16), w_branch_b[0].astype(BF16),
                        w_out[0].astype(BF16), norm2_g[0], peer_wq[0].astype(BF16))
    idx_t, gate_t = _peer_topk(qp, peer_subkeys[0].astype(BF16))
    rows = idx_t.T * PEER_ROW_WORDS
    gate = gate_t.T
    gate_wide = jnp.stack([gate, jnp.zeros_like(gate)], axis=-1).reshape(T, WIDE)
    a_wide = _peer_hid(rows, _pack_table(peer_u[0]), h2.reshape(T, 8, 128), gate_wide)
    out = _peer_out(rows, a_wide, _pack_table(peer_v[0]), x1.reshape(T, 8, 128))
    return out.reshape(B, S, D)
```
